```python
import functools
import jax
import jax.numpy as jnp
from jax import lax
import numpy as np

D_MODEL = 1024
BATCH = 1
SEQ = 16384
DEPTH = 2
DEC_BATCH = 32
DEC_SEQ = 4
PAST_LEN = 16384
PAGE_SIZE = 128

N_BRANCH = 4
BRANCH_W = D_MODEL
GLA_HEADS = 4
GLA_DK = D_MODEL // (2 * GLA_HEADS)
GLA_DV = D_MODEL // GLA_HEADS
GLA_RANK = 16
GLA_TAU = 16.0
GLA_CHUNK = 64
DSA_HEADS = 8
DSA_KV_HEADS = 4
DSA_HD = D_MODEL // DSA_HEADS
IDX_HEADS = 8
IDX_D = 64
TOPK_MAX = 256
Q_BLOCK = 128
LRU_W = D_MODEL
LRU_BLOCKS = 4
LRU_BW = LRU_W // LRU_BLOCKS
CONV_W = 4
LRU_C = 8.0
N_MEM = 256
MEM_HEADS = 4
MEM_HD = D_MODEL // MEM_HEADS
N_EXPERTS = 256
TOP_K = 8
N_GROUPS = 8
TOPK_GROUPS = 4
D_EXPERT = D_MODEL // 4
D_SHARED = D_MODEL // 4
ROUTED_SCALE = 2.5
MOE_BLOCK = 128
MOE_BLOCK_MIN = 8
ROPE_THETA = 10000.0
LN_EPS = 1e-5
ALPHA = (2 * DEPTH) ** 0.25
BETA = (8 * DEPTH) ** -0.25

IN_SIZES = (GLA_HEADS * GLA_DK, GLA_HEADS * GLA_DK, GLA_HEADS * GLA_DV, GLA_HEADS * GLA_DV, GLA_RANK,
            DSA_HEADS * DSA_HD, DSA_KV_HEADS * DSA_HD, DSA_KV_HEADS * DSA_HD,
            IDX_HEADS * IDX_D, IDX_D, IDX_HEADS,
            LRU_W, LRU_W,
            MEM_HEADS * MEM_HD,
            N_BRANCH * D_MODEL)
N_IN = sum(IN_SIZES)

kernel_name = 'hybrid_gla_dsa_rglru_mem_moe_step'


def layer_norm(x, g, b):
    xf = x.astype(jnp.float32)
    mu = jnp.mean(xf, -1, keepdims=True)
    var = jnp.mean(jnp.square(xf - mu), -1, keepdims=True)
    return ((xf - mu) * lax.rsqrt(var + LN_EPS) * g.astype(jnp.float32) + b.astype(jnp.float32)).astype(x.dtype)


def rms_norm(x, g):
    xf = x.astype(jnp.float32)
    return xf * lax.rsqrt(jnp.mean(jnp.square(xf), -1, keepdims=True) + LN_EPS) * g.astype(jnp.float32)


def rope(x, pos):
    half = x.shape[-1] // 2
    inv = ROPE_THETA ** (-jnp.arange(half, dtype=jnp.float32) / half)
    ang = pos.astype(jnp.float32)[:, None] * inv
    cos = jnp.cos(ang)[:, None, :]
    sin = jnp.sin(ang)[:, None, :]
    xf = x.astype(jnp.float32)
    x1, x2 = xf[..., :half], xf[..., half:]
    return jnp.concatenate([x1 * cos - x2 * sin, x2 * cos + x1 * sin], -1).astype(x.dtype)


def split_cols(z, sizes):
    out, start = [], 0
    for s in sizes:
        out.append(z[..., start:start + s])
        start += s
    return out


def gla_chunked(q, k, v, glog, s0):
    B, T, H, DK = q.shape
    DV = v.shape[-1]
    C = GLA_CHUNK if T % GLA_CHUNK == 0 else T
    n = T // C
    to_chunks = lambda a: jnp.moveaxis(a.astype(jnp.float32).reshape(B, n, C, *a.shape[2:]), 1, 0)
    causal = jnp.tril(jnp.ones((C, C), bool))[None, :, :, None, None]

    def step(S, inp):
        qc, kc, vc, gc = inp
        b = jnp.cumsum(gc, axis=1)
        o_inter = jnp.einsum('bchk,bhkv->bchv', qc * jnp.exp(b), S)
        decay = jnp.exp(jnp.where(causal, b[:, :, None] - b[:, None, :], -jnp.inf))
        A = jnp.sum(qc[:, :, None] * kc[:, None] * decay, -1)
        o_intra = jnp.einsum('btsh,bshv->bthv', A, vc)
        bl = b[:, -1]
        S_new = jnp.exp(bl)[..., None] * S + jnp.einsum('bshk,bshv->bhkv', kc * jnp.exp(bl[:, None] - b), vc)
        return S_new, o_inter + o_intra

    S, o = lax.scan(step, s0, (to_chunks(q), to_chunks(k), to_chunks(v), to_chunks(glog)))
    return jnp.moveaxis(o, 0, 1).reshape(B, T, H, DV), S


def dsa_block(q, qi, wi, pos, kidx, gather, k_sel):
    B, Tb, H, hd = q.shape
    L = kidx.shape[1]
    sc = jnp.einsum('bthd,bsd->bths', qi.astype(jnp.float32), kidx.astype(jnp.float32)) * (IDX_D ** -0.5)
    score = jnp.einsum('bths,bth->bts', jax.nn.relu(sc), wi.astype(jnp.float32))
    adm = jnp.arange(L)[None, :] <= pos[:, None]
    score = jnp.where(adm[None], score, -jnp.inf)
    _, sel = lax.top_k(score, k_sel)
    valid = sel <= pos[None, :, None]
    kg, vg = gather(sel)
    qg = q.reshape(B, Tb, DSA_KV_HEADS, H // DSA_KV_HEADS, hd)
    s = jnp.einsum('btngd,btsnd->btngs', qg.astype(jnp.float32), kg.astype(jnp.float32)) * (hd ** -0.5)
    s = jnp.where(valid[:, :, None, None, :], s, -jnp.inf)
    p = jax.nn.softmax(s, axis=-1)
    o = jnp.einsum('btngs,btsnd->btngd', p, vg.astype(jnp.float32))
    return o.reshape(B, Tb, H * hd)


def dsa_sweep(q, qi, wi, pos, kidx, gather, k_sel):
    B, T = q.shape[:2]
    qb = Q_BLOCK if T % Q_BLOCK == 0 else T
    nb = T // qb
    if nb == 1:
        return dsa_block(q, qi, wi, pos, kidx, gather, k_sel)
    split = lambda a: jnp.moveaxis(a.reshape(B, nb, qb, *a.shape[2:]), 1, 0)
    o = lax.map(lambda a: dsa_block(a[0], a[1], a[2], a[3], kidx, gather, k_sel),
                (split(q), split(qi), split(wi), pos.reshape(nb, qb)))
    return jnp.moveaxis(o, 0, 1).reshape(B, T, -1)


def dsa_prompt(q, k, v, qi, ki, wi, pos):
    T = q.shape[1]
    k_sel = min(TOPK_MAX, T // 4)
    take = lambda a, sel: jax.vmap(lambda r, s: r[s])(a, sel)
    gather = lambda sel: (take(k, sel), take(v, sel))
    return dsa_sweep(q, qi, wi, pos, ki, gather, k_sel)


def dsa_sample(q, k, v, qi, ki, wi, pos, ck, cv, cki, page_table):
    B, T = q.shape[:2]
    past = page_table.shape[1] * PAGE_SIZE
    k_sel = min(TOPK_MAX, (past + T) // 4)
    ki_past = cki[page_table].reshape(B, past, IDX_D).astype(ki.dtype)
    kidx = jnp.concatenate([ki_past, ki], axis=1)
    take = lambda a, sel: jax.vmap(lambda r, s: r[s])(a, sel)

    def gather(sel):
        sp = jnp.minimum(sel, past - 1)
        phys = take(page_table, sp // PAGE_SIZE)
        slot = sp % PAGE_SIZE
        sn = jnp.clip(sel - past, 0, T - 1)
        is_new = (sel >= past)[..., None, None]
        kg = jnp.where(is_new, take(k, sn), ck[phys, slot].astype(k.dtype))
        vg = jnp.where(is_new, take(v, sn), cv[phys, slot].astype(v.dtype))
        return kg, vg

    return dsa_sweep(q, qi, wi, pos, kidx, gather, k_sel)


def rglru(xb, yb, pos, conv_w, conv_b, wa, ba, wx, bx, lam, conv0, h0):
    f32 = jnp.float32
    B, T, W = xb.shape
    xp = jnp.concatenate([conv0.astype(xb.dtype), xb], axis=1)
    xpf = xp.astype(f32)
    xc = conv_b.astype(f32)
    for i in range(CONV_W):
        xc = xc + xpf[:, i:i + T] * conv_w[i].astype(f32)
    new_conv = xp[:, xp.shape[1] - (CONV_W - 1):]
    xblk = xc.reshape(B, T, LRU_BLOCKS, LRU_BW)
    gate_r = jax.nn.sigmoid(jnp.einsum('btnj,njk->btnk', xblk, wa.astype(f32)).reshape(B, T, W) + ba.astype(f32))
    gate_i = jax.nn.sigmoid(jnp.einsum('btnj,njk->btnk', xblk, wx.astype(f32)).reshape(B, T, W) + bx.astype(f32))
    log_a = -LRU_C * gate_r * jax.nn.softplus(-lam.astype(f32))
    a = jnp.exp(log_a)
    mult = jnp.sqrt(-jnp.expm1(2.0 * log_a))
    mult = jnp.where((pos == 0)[None, :, None], 1.0, mult)
    bt = mult * gate_i * xc
    bt = bt.at[:, 0].add(a[:, 0] * h0.astype(f32))
    comb = lambda lhs, rhs: (lhs[0] * rhs[0], rhs[0] * lhs[1] + rhs[1])
    _, h = lax.associative_scan(comb, (a, bt), axis=1)
    y = h * jax.nn.gelu(yb.astype(f32))
    return y, new_conv, h[:, -1]


def mem_attend(q, mk, mv):
    s = jnp.einsum('bthd,bmhd->bhtm', q.astype(jnp.float32), mk.astype(jnp.float32)) * (MEM_HD ** -0.5)
    p = jax.nn.softmax(s, axis=-1)
    return jnp.einsum('bhtm,bmhd->bthd', p, mv.astype(jnp.float32))


def mixer(u, pos, lw, mem_k, mem_v, gla_s0, conv0, h0, dsa_fn):
    f32 = jnp.float32
    B, T, _ = u.shape
    z = u @ lw['w_in']
    (gq, gk, gv, gr, glr, dq, dk, dv, iq, ik, iw, lx, ly, mq, gl) = split_cols(z, IN_SIZES)
    q = gq.reshape(B, T, GLA_HEADS, GLA_DK) * (GLA_DK ** -0.5)
    k = gk.reshape(B, T, GLA_HEADS, GLA_DK)
    v = gv.reshape(B, T, GLA_HEADS, GLA_DV)
    glog = jax.nn.log_sigmoid((glr @ lw['gla_w_gate2'] + lw['gla_b_gate']).astype(f32)) / GLA_TAU
    o, s_new = gla_chunked(q, k, v, glog.reshape(B, T, GLA_HEADS, GLA_DK), gla_s0.astype(f32))
    o_gla = (rms_norm(o, lw['gla_norm_g']).reshape(B, T, -1) * jax.nn.silu(gr.astype(f32))).astype(u.dtype)
    qd = rope(dq.reshape(B, T, DSA_HEADS, DSA_HD), pos)
    kd = rope(dk.reshape(B, T, DSA_KV_HEADS, DSA_HD), pos)
    vd = dv.reshape(B, T, DSA_KV_HEADS, DSA_HD)
    qi = rope(iq.reshape(B, T, IDX_HEADS, IDX_D), pos)
    ki = rope(layer_norm(ik, lw['idx_ln_g'], lw['idx_ln_b'])[:, :, None], pos)[:, :, 0]
    wi = iw * (IDX_HEADS ** -0.5)
    o_dsa = dsa_fn(qd, kd, vd, qi, ki, wi, pos).astype(u.dtype)
    o_lru, conv_new, h_new = rglru(lx, ly, pos, lw['conv_w'], lw['conv_b'], lw['lru_wa'], lw['lru_ba'],
                                   lw['lru_wx'], lw['lru_bx'], lw['lru_lambda'], conv0, h0)
    o_lru = o_lru.astype(u.dtype)
    o_mem = mem_attend(mq.reshape(B, T, MEM_HEADS, MEM_HD), mem_k, mem_v).reshape(B, T, -1).astype(u.dtype)
    br = jnp.stack([o_gla, o_dsa, o_lru, o_mem], axis=2)
    proj = jnp.einsum('btjc,jcd->btjd', br, lw['w_branch'])
    gates = jax.nn.sigmoid((gl.reshape(B, T, N_BRANCH, D_MODEL) + lw['b_gate']).astype(f32))
    merged = jnp.sum(gates * proj.astype(f32), axis=2).astype(u.dtype)
    y = merged @ lw['w_out']
    return y, (kd, vd, ki, s_new, conv_new, h_new)


def moe_routed(xt, eidx, wsel, w1, w3, w2):
    N, D = xt.shape
    E = w1.shape[0]
    M = N * TOP_K
    blk = MOE_BLOCK if M >= MOE_BLOCK * E else MOE_BLOCK_MIN
    n_blocks = (M + E * (blk - 1) + blk - 1) // blk
    P = n_blocks * blk
    flat_e = eidx.reshape(M)
    order = jnp.argsort(flat_e)
    se = flat_e[order]
    st = order // TOP_K
    sw = wsel.reshape(M)[order]
    counts = jnp.bincount(flat_e, length=E)
    pcounts = (counts + blk - 1) // blk * blk
    starts = jnp.cumsum(counts) - counts
    pends = jnp.cumsum(pcounts)
    pstarts = pends - pcounts
    dest = pstarts[se] + (jnp.arange(M) - starts[se])
    xp = jnp.zeros((P, D), xt.dtype).at[dest].set(xt[st])
    block_e = jnp.minimum(jnp.searchsorted(pends, jnp.arange(n_blocks) * blk, side='right'), E - 1)

    def expert_block(args):
        xb, e = args
        h = jax.nn.silu(xb @ w1[e]) * (xb @ w3[e])
        return h @ w2[e]

    yp = lax.map(expert_block, (xp.reshape(n_blocks, blk, D), block_e)).reshape(P, D)
    return jax.ops.segment_sum(yp[dest] * sw[:, None], st, num_segments=N)


def moe(u, lw):
    f32 = jnp.float32
    B, T, D = u.shape
    N = B * T
    xt = u.reshape(N, D)
    s = jax.nn.sigmoid(xt.astype(f32) @ lw['router_w'].astype(f32))
    sb = s + lw['router_bias'].astype(f32)
    gscore = jnp.sum(lax.top_k(sb.reshape(N, N_GROUPS, N_EXPERTS // N_GROUPS), 2)[0], -1)
    _, gidx = lax.top_k(gscore, TOPK_GROUPS)
    gmask = jnp.any(gidx[:, :, None] == jnp.arange(N_GROUPS)[None, None, :], axis=1)
    emask = jnp.repeat(gmask, N_EXPERTS // N_GROUPS, axis=1)
    _, eidx = lax.top_k(jnp.where(emask, sb, -jnp.inf), TOP_K)
    wsel = jnp.take_along_axis(s, eidx, axis=1)
    wsel = wsel / jnp.sum(wsel, -1, keepdims=True) * ROUTED_SCALE
    routed = moe_routed(xt, eidx, wsel.astype(xt.dtype), lw['exp_w1'], lw['exp_w3'], lw['exp_w2'])
    shared = (jax.nn.silu(xt @ lw['sh_w1']) * (xt @ lw['sh_w3'])) @ lw['sh_w2']
    return (routed + shared).reshape(B, T, D)


def layer(x, pos, lw, mem_k, mem_v, gla_s0, conv0, h0, dsa_fn):
    m, st = mixer(x, pos, lw, mem_k, mem_v, gla_s0, conv0, h0, dsa_fn)
    x = layer_norm(ALPHA * x + m, lw['ln1_g'], lw['ln1_b'])
    x = layer_norm(ALPHA * x + moe(x, lw), lw['ln2_g'], lw['ln2_b'])
    return x, st


def setup_inputs(seed: int = 0) -> dict:
    key = jax.random.key(seed)
    ks = iter(jax.random.split(key, 64))
    f32 = jnp.float32
    nrm = lambda shape, scale: jax.random.normal(next(ks), shape, f32) * scale
    L = DEPTH
    n_pages = PAST_LEN // PAGE_SIZE
    n_pool = (DEC_BATCH * n_pages * 5) // 4
    x_prompt = nrm((BATCH, SEQ, D_MODEL), 1.0)
    x_sample = nrm((DEC_BATCH, DEC_SEQ, D_MODEL), 1.0)
    cache_k = nrm((L, n_pool, PAGE_SIZE, DSA_KV_HEADS, DSA_HD), 1.0)
    cache_v = nrm((L, n_pool, PAGE_SIZE, DSA_KV_HEADS, DSA_HD), 1.0)
    cache_idx_k = nrm((L, n_pool, PAGE_SIZE, IDX_D), 1.0)
    cache_mem_k = nrm((L, DEC_BATCH, N_MEM, MEM_HEADS, MEM_HD), 1.0)
    cache_mem_v = nrm((L, DEC_BATCH, N_MEM, MEM_HEADS, MEM_HD), 1.0)
    state_gla = nrm((L, DEC_BATCH, GLA_HEADS, GLA_DK, GLA_DV), 1.0)
    state_conv = nrm((L, DEC_BATCH, CONV_W - 1, LRU_W), 1.0)
    state_lru = nrm((L, DEC_BATCH, LRU_W), 0.5)
    page_table = jax.random.permutation(next(ks), n_pool)[:DEC_BATCH * n_pages].reshape(DEC_BATCH, n_pages).astype(jnp.int32)
    mem_prompt = nrm((BATCH, N_MEM, D_MODEL), 1.0)
    ln_in_g = 1.0 + nrm((D_MODEL,), 0.02)
    ln_in_b = nrm((D_MODEL,), 0.02)
    w_in = nrm((L, D_MODEL, N_IN), D_MODEL ** -0.5)
    b_gate = nrm((L, N_BRANCH, D_MODEL), 0.02)
    gla_w_gate2 = nrm((L, GLA_RANK, GLA_HEADS * GLA_DK), GLA_RANK ** -0.5)
    gla_b_gate = nrm((L, GLA_HEADS * GLA_DK), 0.1)
    gla_norm_g = 1.0 + nrm((L, GLA_DV), 0.02)
    idx_ln_g = 1.0 + nrm((L, IDX_D), 0.02)
    idx_ln_b = nrm((L, IDX_D), 0.02)
    conv_w = nrm((L, CONV_W, LRU_W), CONV_W ** -0.5)
    conv_b = nrm((L, LRU_W), 0.02)
    lru_wa = nrm((L, LRU_BLOCKS, LRU_BW, LRU_BW), LRU_BW ** -0.5)
    lru_ba = nrm((L, LRU_W), 0.02)
    lru_wx = nrm((L, LRU_BLOCKS, LRU_BW, LRU_BW), LRU_BW ** -0.5)
    lru_bx = nrm((L, LRU_W), 0.02)
    a_c = jax.random.uniform(next(ks), (L, LRU_W), f32, 0.9, 0.999)
    a0 = a_c ** (1.0 / LRU_C)
    lru_lambda = jnp.log(a0) - jnp.log1p(-a0)
    mem_w_kv = nrm((L, D_MODEL, 2 * MEM_HEADS * MEM_HD), D_MODEL ** -0.5)
    w_branch = nrm((L, N_BRANCH, BRANCH_W, D_MODEL), BETA * BRANCH_W ** -0.5)
    w_out = nrm((L, D_MODEL, D_MODEL), BETA * D_MODEL ** -0.5)
    ln1_g = 1.0 + nrm((L, D_MODEL), 0.02)
    ln1_b = nrm((L, D_MODEL), 0.02)
    ln2_g = 1.0 + nrm((L, D_MODEL), 0.02)
    ln2_b = nrm((L, D_MODEL), 0.02)
    router_w = nrm((L, D_MODEL, N_EXPERTS), D_MODEL ** -0.5)
    router_bias = nrm((L, N_EXPERTS), 0.01)
    exp_w1 = nrm((L, N_EXPERTS, D_MODEL, D_EXPERT), D_MODEL ** -0.5)
    exp_w3 = nrm((L, N_EXPERTS, D_MODEL, D_EXPERT), D_MODEL ** -0.5)
    exp_w2 = nrm((L, N_EXPERTS, D_EXPERT, D_MODEL), BETA * D_EXPERT ** -0.5)
    sh_w1 = nrm((L, D_MODEL, D_SHARED), D_MODEL ** -0.5)
    sh_w3 = nrm((L, D_MODEL, D_SHARED), D_MODEL ** -0.5)
    sh_w2 = nrm((L, D_SHARED, D_MODEL), BETA * D_SHARED ** -0.5)
    return {'x_prompt': x_prompt, 'x_sample': x_sample, 'cache_k': cache_k, 'cache_v': cache_v,
            'cache_idx_k': cache_idx_k, 'cache_mem_k': cache_mem_k, 'cache_mem_v': cache_mem_v,
            'state_gla': state_gla, 'state_conv': state_conv, 'state_lru': state_lru,
            'page_table': page_table, 'mem_prompt': mem_prompt,
            'ln_in_g': ln_in_g, 'ln_in_b': ln_in_b, 'w_in': w_in, 'b_gate': b_gate,
            'gla_w_gate2': gla_w_gate2, 'gla_b_gate': gla_b_gate, 'gla_norm_g': gla_norm_g,
            'idx_ln_g': idx_ln_g, 'idx_ln_b': idx_ln_b, 'conv_w': conv_w, 'conv_b': conv_b,
            'lru_wa': lru_wa, 'lru_ba': lru_ba, 'lru_wx': lru_wx, 'lru_bx': lru_bx, 'lru_lambda': lru_lambda,
            'mem_w_kv': mem_w_kv, 'w_branch': w_branch, 'w_out': w_out,
            'ln1_g': ln1_g, 'ln1_b': ln1_b, 'ln2_g': ln2_g, 'ln2_b': ln2_b,
            'router_w': router_w, 'router_bias': router_bias,
            'exp_w1': exp_w1, 'exp_w3': exp_w3, 'exp_w2': exp_w2,
            'sh_w1': sh_w1, 'sh_w3': sh_w3, 'sh_w2': sh_w2}


def reference(x_prompt, x_sample, cache_k, cache_v, cache_idx_k, cache_mem_k, cache_mem_v,
              state_gla, state_conv, state_lru, page_table, mem_prompt,
              ln_in_g, ln_in_b, w_in, b_gate, gla_w_gate2, gla_b_gate, gla_norm_g, idx_ln_g, idx_ln_b,
              conv_w, conv_b, lru_wa, lru_ba, lru_wx, lru_bx, lru_lambda, mem_w_kv, w_branch, w_out,
              ln1_g, ln1_b, ln2_g, ln2_b, router_w, router_bias, exp_w1, exp_w3, exp_w2,
              sh_w1, sh_w3, sh_w2):
    f32 = jnp.float32
    B, S, _ = x_prompt.shape
    Td = x_sample.shape[1]
    past = page_table.shape[1] * PAGE_SIZE
    pos_p = jnp.arange(S, dtype=jnp.int32)
    pos_s = past + jnp.arange(Td, dtype=jnp.int32)
    hp = layer_norm(x_prompt, ln_in_g, ln_in_b)
    hs = layer_norm(x_sample, ln_in_g, ln_in_b)
    st_prompt, st_sample, mem_new = [], [], []
    for l in range(DEPTH):
        lw = {'w_in': w_in[l], 'b_gate': b_gate[l], 'gla_w_gate2': gla_w_gate2[l], 'gla_b_gate': gla_b_gate[l],
              'gla_norm_g': gla_norm_g[l], 'idx_ln_g': idx_ln_g[l], 'idx_ln_b': idx_ln_b[l],
              'conv_w': conv_w[l], 'conv_b': conv_b[l], 'lru_wa': lru_wa[l], 'lru_ba': lru_ba[l],
              'lru_wx': lru_wx[l], 'lru_bx': lru_bx[l], 'lru_lambda': lru_lambda[l],
              'w_branch': w_branch[l], 'w_out': w_out[l],
              'ln1_g': ln1_g[l], 'ln1_b': ln1_b[l], 'ln2_g': ln2_g[l], 'ln2_b': ln2_b[l],
              'router_w': router_w[l], 'router_bias': router_bias[l],
              'exp_w1': exp_w1[l], 'exp_w3': exp_w3[l], 'exp_w2': exp_w2[l],
              'sh_w1': sh_w1[l], 'sh_w3': sh_w3[l], 'sh_w2': sh_w2[l]}
        mkv = (mem_prompt @ mem_w_kv[l]).reshape(B, N_MEM, 2, MEM_HEADS, MEM_HD)
        mk_l, mv_l = mkv[:, :, 0], mkv[:, :, 1]
        hp, stp = layer(hp, pos_p, lw, mk_l, mv_l,
                        jnp.zeros((B, GLA_HEADS, GLA_DK, GLA_DV), f32),
                        jnp.zeros((B, CONV_W - 1, LRU_W), hp.dtype),
                        jnp.zeros((B, LRU_W), f32), dsa_prompt)
        dsa_s = functools.partial(dsa_sample, ck=cache_k[l], cv=cache_v[l], cki=cache_idx_k[l], page_table=page_table)
        hs, sts = layer(hs, pos_s, lw, cache_mem_k[l], cache_mem_v[l],
                        state_gla[l], state_conv[l], state_lru[l], dsa_s)
        st_prompt.append(stp)
        st_sample.append(sts)
        mem_new.append((mk_l, mv_l))
    stk = lambda seq, j: jnp.stack([t[j] for t in seq], axis=0)
    k_p, v_p, ik_p, gla_p, conv_p, lru_p = [stk(st_prompt, j) for j in range(6)]
    k_s, v_s, ik_s, gla_s, conv_s, lru_s = [stk(st_sample, j) for j in range(6)]
    memk_p, memv_p = stk(mem_new, 0), stk(mem_new, 1)
    return (hp, hs, k_p, v_p, ik_p, gla_p, conv_p, lru_p, memk_p, memv_p, k_s, v_s, ik_s, gla_s, conv_s, lru_s)
```

```python
import functools
import math

import jax
import jax.numpy as jnp
from jax import lax
from jax.experimental import pallas as pl
from jax.experimental.pallas import tpu as pltpu

F32 = jnp.float32
BF16 = jnp.bfloat16
I32 = jnp.int32

D_MODEL = 1024
DEPTH_ALPHA_POW = 0.25
PAGE_SIZE = 128
N_BRANCH = 4
GLA_HEADS = 4
GLA_DK = 128
GLA_DV = 256
GLA_RANK = 16
GLA_TAU = 16.0
GLA_CHUNK = 64
DSA_HEADS = 8
DSA_KV_HEADS = 4
DSA_HD = 128
IDX_HEADS = 8
IDX_D = 64
TOPK_MAX = 256
LRU_W = 1024
LRU_BLOCKS = 4
LRU_BW = LRU_W // LRU_BLOCKS
CONV_W = 4
LRU_C = 8.0
N_MEM = 256
MEM_HEADS = 4
MEM_HD = 256
N_EXPERTS = 256
TOP_K = 8
N_GROUPS = 8
TOPK_GROUPS = 4
ROUTED_SCALE = 2.5
ROPE_THETA = 10000.0
LN_EPS = 1e-5

IN_SIZES = (GLA_HEADS * GLA_DK, GLA_HEADS * GLA_DK, GLA_HEADS * GLA_DV, GLA_HEADS * GLA_DV, GLA_RANK,
            DSA_HEADS * DSA_HD, DSA_KV_HEADS * DSA_HD, DSA_KV_HEADS * DSA_HD,
            IDX_HEADS * IDX_D, IDX_D, IDX_HEADS,
            LRU_W, LRU_W, MEM_HEADS * MEM_HD, N_BRANCH * D_MODEL)

LANE = 128
Z_GQ, Z_GK, Z_GV, Z_GR = 0, 512, 1024, 2048
Z_DQ, Z_DK, Z_DV, Z_IQ = 3072, 4096, 4608, 5120
Z_GLR, Z_MISC = 5632, 5760
Z_LX, Z_LY, Z_MQ, Z_GL = 6144, 7168, 8192, 9216
Z_W = 13312

VMEM_LIMIT = 56 * 1024 * 1024
MOE_BLK = 256
INT_MIN = -2 ** 31


def _tile(n, target):
    if n <= target:
        return n
    for t in range(target, 7, -1):
        if n % t == 0 and t % 8 == 0:
            return t
    return n


def _cparams(sem, vmem=None):
    return pltpu.CompilerParams(dimension_semantics=sem, vmem_limit_bytes=vmem or VMEM_LIMIT)


def _ln_rows(x, g, b):
    mu = jnp.mean(x, -1, keepdims=True)
    xc = x - mu
    var = jnp.mean(xc * xc, -1, keepdims=True)
    return xc * lax.rsqrt(var + LN_EPS) * g + b


def _sigmoid(x):
    return 1.0 / (1.0 + jnp.exp(-x))


def _silu(x):
    return x * _sigmoid(x)


def _dot(a, b):
    return jnp.dot(a, b, preferred_element_type=F32)


def _dot_nt(a, b):
    return lax.dot_general(a, b, (((1,), (1,)), ((), ())), preferred_element_type=F32)


def _dot_tn(a, b):
    return lax.dot_general(a, b, (((0,), (0,)), ((), ())), preferred_element_type=F32)


def _ln_kernel(x_ref, g_ref, b_ref, o_ref):
    o_ref[...] = _ln_rows(x_ref[...], g_ref[...], b_ref[...])


def layer_norm_rows(x, g, b):
    m, d = x.shape
    tm = _tile(m, 512)
    return pl.pallas_call(
        _ln_kernel,
        out_shape=jax.ShapeDtypeStruct((m, d), F32),
        grid=(m // tm,),
        in_specs=[pl.BlockSpec((tm, d), lambda i: (i, 0)),
                  pl.BlockSpec((1, d), lambda i: (0, 0)),
                  pl.BlockSpec((1, d), lambda i: (0, 0))],
        out_specs=pl.BlockSpec((tm, d), lambda i: (i, 0)),
        compiler_params=_cparams(("parallel",)),
        name="ln_rows",
    )(x, g.reshape(1, d), b.reshape(1, d))


def _mm_kernel(x_ref, w_ref, o_ref, xb_ref):
    @pl.when(pl.program_id(1) == 0)
    def _():
        xb_ref[...] = x_ref[...].astype(BF16)

    o_ref[...] = _dot(xb_ref[...], w_ref[...])


def matmul(x, w):
    m, k = x.shape
    n = w.shape[1]
    tm = _tile(m, 1024)
    tn = _tile(n, 1024)
    return pl.pallas_call(
        _mm_kernel,
        out_shape=jax.ShapeDtypeStruct((m, n), F32),
        grid=(m // tm, n // tn),
        in_specs=[pl.BlockSpec((tm, k), lambda i, j: (i, 0)),
                  pl.BlockSpec((k, tn), lambda i, j: (0, j))],
        out_specs=pl.BlockSpec((tm, tn), lambda i, j: (i, j)),
        scratch_shapes=[pltpu.VMEM((tm, k), BF16)],
        compiler_params=_cparams(("parallel", "arbitrary")),
        name="matmul",
    )(x, w)


def _rot_half(x, width):
    if width == LANE:
        return pltpu.roll(x, LANE // 2, 1)
    half = width // 2
    lane = lax.broadcasted_iota(I32, x.shape, 1)
    first = (lane % width) < half
    return jnp.where(first, pltpu.roll(x, LANE - half, 1), pltpu.roll(x, half, 1))


def _prep_kernel(dq_ref, dk_ref, dv_ref, iq_ref, misc_ref, c128_ref, s128_ref, c64_ref, s64_ref,
                 ig_ref, ib_ref,
                 q_ref, k_ref, kb_ref, v_ref, vb_ref, qi_ref, ki_ref, kib_ref, wi_ref):
    c128, s128 = c128_ref[...], s128_ref[...]
    c64, s64 = c64_ref[...], s64_ref[...]
    for h in range(DSA_HEADS):
        sl = slice(h * LANE, (h + 1) * LANE)
        x = dq_ref[:, sl]
        q_ref[:, sl] = (x * c128 + _rot_half(x, LANE) * s128).astype(BF16)
    for h in range(DSA_KV_HEADS):
        sl = slice(h * LANE, (h + 1) * LANE)
        x = dk_ref[:, sl]
        r = x * c128 + _rot_half(x, LANE) * s128
        k_ref[:, sl] = r
        kb_ref[:, sl] = r.astype(BF16)
    v = dv_ref[...]
    v_ref[...] = v
    vb_ref[...] = v.astype(BF16)
    for c in range(IDX_HEADS * IDX_D // LANE):
        sl = slice(c * LANE, (c + 1) * LANE)
        x = iq_ref[:, sl]
        qi_ref[:, sl] = (x * c64 + _rot_half(x, IDX_D) * s64).astype(BF16)
    misc = misc_ref[...]
    lane = lax.broadcasted_iota(I32, misc.shape, 1)
    isk = lane < IDX_D
    mu = jnp.sum(jnp.where(isk, misc, 0.0), -1, keepdims=True) * (1.0 / IDX_D)
    xc = jnp.where(isk, misc - mu, 0.0)
    var = jnp.sum(xc * xc, -1, keepdims=True) * (1.0 / IDX_D)
    kn = xc * lax.rsqrt(var + LN_EPS) * ig_ref[...] + ib_ref[...]
    kr = kn * c64 + _rot_half(kn, IDX_D) * s64
    ki_ref[...] = kr[:, :IDX_D]
    kib_ref[...] = kr[:, :IDX_D].astype(BF16)
    w = pltpu.roll(misc, LANE - IDX_D, 1)
    wi_ref[...] = jnp.where(lane < IDX_HEADS, w * (IDX_HEADS ** -0.5) * (IDX_D ** -0.5), 0.0)


def dsa_prep(z, tabs, idx_g, idx_b):
    t = z.shape[0]
    tb = _tile(t, 512)
    c128, s128, c64, s64 = tabs
    zspec = lambda w, off: pl.BlockSpec((tb, w), lambda i, o=off // w: (i, o))
    row = lambda w: pl.BlockSpec((tb, w), lambda i: (i, 0))
    pad = lambda a: jnp.pad(a, (0, LANE - IDX_D)).reshape(1, LANE)
    outs = pl.pallas_call(
        _prep_kernel,
        out_shape=[jax.ShapeDtypeStruct((t, 1024), BF16),
                   jax.ShapeDtypeStruct((t, 512), F32),
                   jax.ShapeDtypeStruct((t, 512), BF16),
                   jax.ShapeDtypeStruct((t, 512), F32),
                   jax.ShapeDtypeStruct((t, 512), BF16),
                   jax.ShapeDtypeStruct((t, 512), BF16),
                   jax.ShapeDtypeStruct((t, IDX_D), F32),
                   jax.ShapeDtypeStruct((t, IDX_D), BF16),
                   jax.ShapeDtypeStruct((t, LANE), F32)],
        grid=(t // tb,),
        in_specs=[zspec(1024, Z_DQ), zspec(512, Z_DK), zspec(512, Z_DV), zspec(512, Z_IQ), zspec(LANE, Z_MISC),
                  row(LANE), row(LANE), row(LANE), row(LANE),
                  pl.BlockSpec((1, LANE), lambda i: (0, 0)), pl.BlockSpec((1, LANE), lambda i: (0, 0))],
        out_specs=[row(1024), row(512), row(512), row(512), row(512), row(512), row(IDX_D), row(IDX_D), row(LANE)],
        compiler_params=_cparams(("parallel",)),
        name="dsa_prep",
    )(z, z, z, z, z, c128, s128, c64, s64, pad(idx_g), pad(idx_b))
    return outs


def rope_tables(pos):
    posf = pos.astype(F32)[:, None]

    def tab(width):
        half = width // 2
        inv = ROPE_THETA ** (-jnp.arange(half, dtype=F32) / half)
        ang = posf * inv
        c, s = jnp.cos(ang), jnp.sin(ang)
        reps = LANE // width
        return jnp.tile(jnp.concatenate([c, c], 1), (1, reps)), jnp.tile(jnp.concatenate([-s, s], 1), (1, reps))

    c128, s128 = tab(DSA_HD)
    c64, s64 = tab(IDX_D)
    return c128, s128, c64, s64


def _log_sigmoid(x):
    return jnp.minimum(x, 0.0) - jnp.log1p(jnp.exp(-jnp.abs(x)))


def _gla_kernel(q_ref, k_ref, v_ref, r_ref, glr_ref, w2_ref, bg_ref, ng_ref, o_ref, st_ref, s_ref, *, tb):
    c = GLA_CHUNK
    j = pl.program_id(1)

    @pl.when(j == 0)
    def _():
        s_ref[...] = jnp.zeros_like(s_ref)

    ri = lax.broadcasted_iota(I32, (c, c), 0)
    ci = lax.broadcasted_iota(I32, (c, c), 1)
    causal = ci <= ri
    tri = causal.astype(F32)
    w2 = w2_ref[...]
    bg = bg_ref[...]
    ng = ng_ref[...]

    def chunk(ic, carry):
        r0 = pl.multiple_of(ic * c, c)
        q = q_ref[pl.ds(r0, c), :] * (GLA_DK ** -0.5)
        k = k_ref[pl.ds(r0, c), :]
        v = v_ref[pl.ds(r0, c), :].astype(BF16)
        x = _dot(glr_ref[pl.ds(r0, c), :].astype(BF16), w2) + bg
        g = _log_sigmoid(x) * (1.0 / GLA_TAU)
        b = jnp.dot(tri, g, preferred_element_type=F32, precision=lax.Precision.HIGHEST)
        bm = b[c // 2 - 1:c // 2, :]
        bl = b[c - 1:c, :]
        st = s_ref[...]
        qe = (q * jnp.exp(b - bm)).astype(BF16)
        ke = (k * jnp.exp(bm - b)).astype(BF16)
        a = jnp.where(causal, _dot_nt(qe, ke), 0.0)
        o = _dot_nt((q * jnp.exp(b)).astype(BF16), st.astype(BF16)) + _dot(a.astype(BF16), v)
        kd = (k * jnp.exp(bl - b)).astype(BF16)
        s_ref[...] = st * jnp.exp(bl) + _dot_tn(v, kd)
        on = o * lax.rsqrt(jnp.mean(o * o, -1, keepdims=True) + LN_EPS) * ng
        o_ref[pl.ds(r0, c), :] = on * _silu(r_ref[pl.ds(r0, c), :])
        return carry

    lax.fori_loop(0, tb // c, chunk, 0)

    @pl.when(j == pl.num_programs(1) - 1)
    def _():
        st_ref[0] = s_ref[...]


def gla_prompt(z, w2, bg, ng):
    t = z.shape[0]
    tb = _tile(t, 512)
    assert tb % GLA_CHUNK == 0
    h = GLA_HEADS
    w2p = jnp.pad(w2, ((0, LANE - GLA_RANK), (0, 0))).astype(BF16)
    o, st = pl.pallas_call(
        functools.partial(_gla_kernel, tb=tb),
        out_shape=[jax.ShapeDtypeStruct((t, h * GLA_DV), F32),
                   jax.ShapeDtypeStruct((h, GLA_DV, GLA_DK), F32)],
        grid=(h, t // tb),
        in_specs=[pl.BlockSpec((tb, GLA_DK), lambda hh, j: (j, Z_GQ // GLA_DK + hh)),
                  pl.BlockSpec((tb, GLA_DK), lambda hh, j: (j, Z_GK // GLA_DK + hh)),
                  pl.BlockSpec((tb, GLA_DV), lambda hh, j: (j, Z_GV // GLA_DV + hh)),
                  pl.BlockSpec((tb, GLA_DV), lambda hh, j: (j, Z_GR // GLA_DV + hh)),
                  pl.BlockSpec((tb, LANE), lambda hh, j: (j, Z_GLR // LANE)),
                  pl.BlockSpec((LANE, GLA_DK), lambda hh, j: (0, hh)),
                  pl.BlockSpec((1, GLA_DK), lambda hh, j: (0, hh)),
                  pl.BlockSpec((1, GLA_DV), lambda hh, j: (0, 0))],
        out_specs=[pl.BlockSpec((tb, GLA_DV), lambda hh, j: (j, hh)),
                   pl.BlockSpec((1, GLA_DV, GLA_DK), lambda hh, j: (hh, 0, 0))],
        scratch_shapes=[pltpu.VMEM((GLA_DV, GLA_DK), F32)],
        compiler_params=_cparams(("parallel", "arbitrary")),
        name="gla_prompt",
    )(z, z, z, z, z, w2p, bg.reshape(1, -1), ng.reshape(1, -1))
    return o, jnp.swapaxes(st, 1, 2)


def _sort_key(x):
    bits = lax.bitcast_convert_type(x + 0.0, I32)
    return bits ^ ((bits >> 31) & 0x7FFFFFFF)


def _dsa_kernel(q_ref, qi_ref, wi_ref, kit_ref, kt_ref, v_ref, o_ref,
                key_ref, tri_ref, m_ref, l_ref, acc_ref, *, tq, tk, k_sel):
    i = pl.program_id(0)

    @pl.when(i == 0)
    def _():
        r = lax.broadcasted_iota(I32, (tk, tk), 0)
        c = lax.broadcasted_iota(I32, (tk, tk), 1)
        tri_ref[...] = (r <= c).astype(BF16)

    t0 = i * tq
    n_kt = (t0 + tq + tk - 1) // tk
    row = t0 + lax.broadcasted_iota(I32, (tq, tk), 0)
    wi = wi_ref[...]

    def score_tile(kt, carry):
        c0 = pl.multiple_of(kt * tk, tk)
        acc = jnp.zeros((tq, tk), F32)
        for h in range(IDX_HEADS):
            sc = _dot(qi_ref[:, h * IDX_D:(h + 1) * IDX_D], kit_ref[:, pl.ds(c0, tk)])
            acc = acc + jnp.maximum(sc, 0.0) * wi[:, h:h + 1]
        col = c0 + lax.broadcasted_iota(I32, (tq, tk), 1)
        key_ref[:, pl.ds(c0, tk)] = jnp.where(col <= row, _sort_key(acc), INT_MIN)
        return carry

    lax.fori_loop(0, n_kt, score_tile, 0)

    def count(pred_fn):
        def body(kt, acc):
            c0 = pl.multiple_of(kt * tk, tk)
            return acc + jnp.sum(pred_fn(key_ref[:, pl.ds(c0, tk)]).astype(F32), axis=1, keepdims=True)
        return lax.fori_loop(0, n_kt, body, jnp.zeros((tq, 1), F32))

    kf = float(k_sel)
    lo = jnp.where(count(lambda kk: kk >= 0) >= kf, 0, INT_MIN).astype(I32)

    def bit_step(it, lo):
        cand = lo + jnp.left_shift(jnp.int32(1), 30 - it)
        return jnp.where(count(lambda kk: kk >= cand) >= kf, cand, lo)

    tau = lax.fori_loop(0, 31, bit_step, lo)
    n_gt = count(lambda kk: kk > tau)
    n_ge = count(lambda kk: kk >= tau)
    need = kf - n_gt
    tau_eff = jnp.maximum(tau, INT_MIN + 1)
    has_tie = jnp.max(jnp.where(tau > INT_MIN, n_ge - n_gt - need, 0.0)) > 0.0

    m_ref[...] = jnp.full_like(m_ref, -1e30)
    l_ref[...] = jnp.zeros_like(l_ref)
    acc_ref[...] = jnp.zeros_like(acc_ref)
    scale = DSA_HD ** -0.5
    g = DSA_HEADS // DSA_KV_HEADS

    def attend(kt, run_eq):
        c0 = pl.multiple_of(kt * tk, tk)
        keys = key_ref[:, pl.ds(c0, tk)]

        def fast(run_eq):
            return (keys >= tau_eff).astype(I32), run_eq

        def slow(run_eq):
            eq = (keys == tau) & (keys > INT_MIN)
            rank = run_eq + _dot(eq.astype(BF16), tri_ref[...])
            sel = (keys > tau) | (eq & (rank <= need))
            return sel.astype(I32), run_eq + jnp.sum(eq.astype(F32), axis=1, keepdims=True)

        mask_i, run_eq = lax.cond(has_tie, slow, fast, run_eq)
        mask = mask_i != 0
        for n in range(DSA_KV_HEADS):
            ktile = kt_ref[n * DSA_HD:(n + 1) * DSA_HD, pl.ds(c0, tk)]
            vtile = v_ref[pl.ds(c0, tk), n * DSA_HD:(n + 1) * DSA_HD]
            for gg in range(g):
                h = n * g + gg
                s = _dot(q_ref[:, h * DSA_HD:(h + 1) * DSA_HD], ktile) * scale
                s = jnp.where(mask, s, -1e30)
                m_old = m_ref[h]
                m_new = jnp.maximum(m_old, jnp.max(s, axis=1, keepdims=True))
                p = jnp.where(mask, jnp.exp(s - m_new), 0.0)
                alpha = jnp.exp(m_old - m_new)
                l_ref[h] = alpha * l_ref[h] + jnp.sum(p, axis=1, keepdims=True)
                acc_ref[h] = alpha * acc_ref[h] + _dot(p.astype(BF16), vtile)
                m_ref[h] = m_new
        return run_eq

    lax.fori_loop(0, n_kt, attend, jnp.zeros((tq, 1), F32))
    for h in range(DSA_HEADS):
        o_ref[:, h * DSA_HD:(h + 1) * DSA_HD] = acc_ref[h] / l_ref[h]


def dsa_prompt(q_bf, qi_bf, wi, kit_bf, kt_bf, v_bf):
    t = q_bf.shape[0]
    tq = _tile(t, 128)
    tk = _tile(t, 512)
    k_sel = min(TOPK_MAX, t // 4)
    whole = lambda shape: pl.BlockSpec(shape, lambda i: (0, 0), pipeline_mode=pl.Buffered(1))
    return pl.pallas_call(
        functools.partial(_dsa_kernel, tq=tq, tk=tk, k_sel=k_sel),
        out_shape=jax.ShapeDtypeStruct((t, DSA_HEADS * DSA_HD), F32),
        grid=(t // tq,),
        in_specs=[pl.BlockSpec((tq, DSA_HEADS * DSA_HD), lambda i: (i, 0)),
                  pl.BlockSpec((tq, IDX_HEADS * IDX_D), lambda i: (i, 0)),
                  pl.BlockSpec((tq, LANE), lambda i: (i, 0)),
                  whole((IDX_D, t)), whole((DSA_KV_HEADS * DSA_HD, t)), whole((t, DSA_KV_HEADS * DSA_HD))],
        out_specs=pl.BlockSpec((tq, DSA_HEADS * DSA_HD), lambda i: (i, 0)),
        scratch_shapes=[pltpu.VMEM((tq, t), I32),
                        pltpu.VMEM((tk, tk), BF16),
                        pltpu.VMEM((DSA_HEADS, tq, 1), F32),
                        pltpu.VMEM((DSA_HEADS, tq, 1), F32),
                        pltpu.VMEM((DSA_HEADS, tq, DSA_HD), F32)],
        compiler_params=_cparams(("arbitrary",)),
        name="dsa_prompt",
    )(q_bf, qi_bf, wi, kit_bf, kt_bf, v_bf)


def _gelu_tanh(x):
    return 0.5 * x * (1.0 + jnp.tanh(math.sqrt(2.0 / math.pi) * (x + 0.044715 * x * x * x)))


def _softplus(x):
    return jnp.maximum(x, 0.0) + jnp.log1p(jnp.exp(-jnp.abs(x)))


def _lru_kernel(x_ref, y_ref, cw_ref, cb_ref, wa_ref, ba_ref, wx_ref, bx_ref, lam_ref,
                o_ref, hl_ref, xp_ref, h_ref, *, tb):
    i = pl.program_id(0)

    @pl.when(i == 0)
    def _():
        xp_ref[0:8, :] = jnp.zeros((8, LRU_W), F32)
        h_ref[...] = jnp.zeros_like(h_ref)

    xp_ref[8:8 + tb, :] = x_ref[...]
    rows = lax.broadcasted_iota(I32, (tb, LRU_BW), 0)
    first = (rows == 0) & (i == 0)
    for n in range(LRU_BLOCKS):
        sl = slice(n * LRU_BW, (n + 1) * LRU_BW)
        xc = jnp.broadcast_to(cb_ref[:, sl], (tb, LRU_BW))
        for w in range(CONV_W):
            xc = xc + xp_ref[8 - (CONV_W - 1) + w:8 - (CONV_W - 1) + w + tb, sl] * cw_ref[w:w + 1, sl]
        xcb = xc.astype(BF16)
        gate_r = _sigmoid(_dot(xcb, wa_ref[n]) + ba_ref[:, sl])
        gate_i = _sigmoid(_dot(xcb, wx_ref[n]) + bx_ref[:, sl])
        log_a = -LRU_C * gate_r * _softplus(-lam_ref[:, sl])
        a = jnp.exp(log_a)
        th = jnp.tanh(log_a)
        mult = jnp.sqrt(-2.0 * th / (1.0 - th))
        mult = jnp.where(first, 1.0, mult)
        b = mult * gate_i * xc
        d = 1
        while d < tb:
            a_sh = pltpu.roll(a, d, 0)
            b_sh = pltpu.roll(b, d, 0)
            ok = rows >= d
            b = jnp.where(ok, a * b_sh + b, b)
            a = jnp.where(ok, a * a_sh, a)
            d *= 2
        h = a * h_ref[:, sl] + b
        h_ref[:, sl] = h[tb - 1:tb, :]
        o_ref[:, sl] = h * _gelu_tanh(y_ref[:, sl])
    xp_ref[0:8, :] = xp_ref[tb:tb + 8, :]
    hl_ref[...] = h_ref[...]


def rglru_prompt(z, conv_w, conv_b, wa, ba, wx, bx, lam):
    t = z.shape[0]
    tb = _tile(t, 256)
    vec = lambda a: a.reshape(1, LRU_W)
    cst = lambda shape: pl.BlockSpec(shape, lambda i: tuple(0 for _ in shape))
    return pl.pallas_call(
        functools.partial(_lru_kernel, tb=tb),
        out_shape=[jax.ShapeDtypeStruct((t, LRU_W), F32), jax.ShapeDtypeStruct((1, LRU_W), F32)],
        grid=(t // tb,),
        in_specs=[pl.BlockSpec((tb, LRU_W), lambda i: (i, Z_LX // LRU_W)),
                  pl.BlockSpec((tb, LRU_W), lambda i: (i, Z_LY // LRU_W)),
                  cst((CONV_W, LRU_W)), cst((1, LRU_W)),
                  cst((LRU_BLOCKS, LRU_BW, LRU_BW)), cst((1, LRU_W)),
                  cst((LRU_BLOCKS, LRU_BW, LRU_BW)), cst((1, LRU_W)), cst((1, LRU_W))],
        out_specs=[pl.BlockSpec((tb, LRU_W), lambda i: (i, 0)), cst((1, LRU_W))],
        scratch_shapes=[pltpu.VMEM((tb + 8, LRU_W), F32), pltpu.VMEM((1, LRU_W), F32)],
        compiler_params=_cparams(("arbitrary",)),
        name="rglru_prompt",
    )(z, z, conv_w, vec(conv_b), wa.astype(BF16), vec(ba), wx.astype(BF16), vec(bx), vec(lam))


def _mem_kernel(q_ref, mk_ref, mv_ref, o_ref):
    scale = MEM_HD ** -0.5
    for h in range(MEM_HEADS):
        sl = slice(h * MEM_HD, (h + 1) * MEM_HD)
        s = _dot_nt(q_ref[:, sl].astype(BF16), mk_ref[:, sl].astype(BF16)) * scale
        p = jnp.exp(s - jnp.max(s, -1, keepdims=True))
        p = p / jnp.sum(p, -1, keepdims=True)
        o_ref[:, sl] = _dot(p.astype(BF16), mv_ref[:, sl].astype(BF16))


def mem_attend_prompt(z, mkv):
    t = z.shape[0]
    tb = _tile(t, 512)
    w = MEM_HEADS * MEM_HD
    return pl.pallas_call(
        _mem_kernel,
        out_shape=jax.ShapeDtypeStruct((t, w), F32),
        grid=(t // tb,),
        in_specs=[pl.BlockSpec((tb, w), lambda i: (i, Z_MQ // w)),
                  pl.BlockSpec((N_MEM, w), lambda i: (0, 0)),
                  pl.BlockSpec((N_MEM, w), lambda i: (0, 1))],
        out_specs=pl.BlockSpec((tb, w), lambda i: (i, 0)),
        compiler_params=_cparams(("parallel",)),
        name="mem_attend",
    )(z, mkv, mkv)


def _merge_kernel(b0_ref, b1_ref, b2_ref, b3_ref, g0_ref, g1_ref, g2_ref, g3_ref, x_ref,
                  wb_ref, bg_ref, wo_ref, lg_ref, lb_ref, o_ref, *, alpha):
    brs = (b0_ref, b1_ref, b2_ref, b3_ref)
    gls = (g0_ref, g1_ref, g2_ref, g3_ref)
    merged = None
    for j in range(N_BRANCH):
        proj = _dot(brs[j][...].astype(BF16), wb_ref[j])
        term = _sigmoid(gls[j][...] + bg_ref[j:j + 1, :]) * proj
        merged = term if merged is None else merged + term
    y = _dot(merged.astype(BF16), wo_ref[...])
    o_ref[...] = _ln_rows(alpha * x_ref[...] + y, lg_ref[...], lb_ref[...])


def merge_out(branches, z, x, wb, bgate, wo, lg, lb, alpha):
    t = x.shape[0]
    tb = _tile(t, 256)
    d = D_MODEL
    row = pl.BlockSpec((tb, d), lambda i: (i, 0))
    cst = lambda shape: pl.BlockSpec(shape, lambda i: tuple(0 for _ in shape))
    gl = [pl.BlockSpec((tb, d), lambda i, o=Z_GL // d + j: (i, o)) for j in range(N_BRANCH)]
    return pl.pallas_call(
        functools.partial(_merge_kernel, alpha=alpha),
        out_shape=jax.ShapeDtypeStruct((t, d), F32),
        grid=(t // tb,),
        in_specs=[row, row, row, row] + gl + [row, cst((N_BRANCH, d, d)), cst((N_BRANCH, d)), cst((d, d)),
                                               cst((1, d)), cst((1, d))],
        out_specs=row,
        compiler_params=_cparams(("parallel",)),
        name="merge_out",
    )(*branches, z, z, z, z, x, wb, bgate, wo, lg.reshape(1, d), lb.reshape(1, d))


def _first_argmax(x, n):
    m = jnp.max(x, axis=0, keepdims=True)
    ri = lax.broadcasted_iota(I32, x.shape, 0)
    idx = jnp.min(jnp.where(x == m, ri, n), axis=0, keepdims=True)
    return m, idx, ri


def _router_kernel(x_ref, rwt_ref, rb_ref, e_ref, w_ref):
    logits = lax.dot_general(rwt_ref[...], x_ref[...], (((1,), (1,)), ((), ())),
                             preferred_element_type=F32, precision=lax.Precision.HIGHEST)
    s = _sigmoid(logits)
    sb = s + rb_ref[...]
    gsz = N_EXPERTS // N_GROUPS
    neg = -jnp.inf
    gs = []
    for g in range(N_GROUPS):
        blk = sb[g * gsz:(g + 1) * gsz, :]
        m1, i1, ri = _first_argmax(blk, gsz)
        m2 = jnp.max(jnp.where(ri == i1, neg, blk), axis=0, keepdims=True)
        gs.append(m1 + m2)
    gscore = jnp.concatenate(gs, axis=0)
    gmask = jnp.zeros(gscore.shape, jnp.bool_)
    for _ in range(TOPK_GROUPS):
        _, gi, ri = _first_argmax(gscore, N_GROUPS)
        hit = ri == gi
        gmask = gmask | hit
        gscore = jnp.where(hit, neg, gscore)
    cand = jnp.concatenate(
        [jnp.where(gmask[g:g + 1, :], sb[g * gsz:(g + 1) * gsz, :], neg) for g in range(N_GROUPS)], axis=0)
    idxs, ws = [], []
    for _ in range(TOP_K):
        _, ei, ri = _first_argmax(cand, N_EXPERTS)
        hit = ri == ei
        idxs.append(ei)
        ws.append(jnp.sum(jnp.where(hit, s, 0.0), axis=0, keepdims=True))
        cand = jnp.where(hit, neg, cand)
    w = jnp.concatenate(ws, axis=0)
    e_ref[...] = jnp.concatenate(idxs, axis=0)
    w_ref[...] = w / jnp.sum(w, axis=0, keepdims=True) * ROUTED_SCALE


def moe_router(x, router_w, router_b):
    t, d = x.shape
    tb = _tile(t, 512) if t % 128 == 0 else t
    while tb % 128 != 0 and tb != t:
        tb = t
    return pl.pallas_call(
        _router_kernel,
        out_shape=[jax.ShapeDtypeStruct((TOP_K, t), I32), jax.ShapeDtypeStruct((TOP_K, t), F32)],
        grid=(t // tb,),
        in_specs=[pl.BlockSpec((tb, d), lambda i: (i, 0)),
                  pl.BlockSpec((N_EXPERTS, d), lambda i: (0, 0)),
                  pl.BlockSpec((N_EXPERTS, 1), lambda i: (0, 0))],
        out_specs=[pl.BlockSpec((TOP_K, tb), lambda i: (0, i)), pl.BlockSpec((TOP_K, tb), lambda i: (0, i))],
        compiler_params=_cparams(("parallel",)),
        name="moe_router",
    )(x, router_w.T, router_b.reshape(N_EXPERTS, 1))


def _experts_kernel(be_ref, nv_ref, x_ref, rw_ref, w1_ref, w3_ref, w2_ref, o_ref, w1b, w3b, w2b):
    i = pl.program_id(0)
    e = be_ref[i]
    prev = be_ref[jnp.maximum(i - 1, 0)]

    @pl.when((i == 0) | (e != prev))
    def _():
        w1b[...] = w1_ref[0].astype(BF16)
        w3b[...] = w3_ref[0].astype(BF16)
        w2b[...] = w2_ref[0].astype(BF16)

    @pl.when(i < nv_ref[0])
    def _():
        x = x_ref[...]
        h = _silu(_dot(x, w1b[...])) * _dot(x, w3b[...])
        o_ref[...] = _dot(h.astype(BF16), w2b[...]) * rw_ref[...]

    @pl.when(i >= nv_ref[0])
    def _():
        o_ref[...] = jnp.zeros_like(o_ref)


def moe_experts(xp, row_w, block_e, n_valid, w1, w3, w2):
    p, d = xp.shape
    de = w1.shape[2]
    nb = p // MOE_BLK
    grid_spec = pltpu.PrefetchScalarGridSpec(
        num_scalar_prefetch=2,
        grid=(nb,),
        in_specs=[pl.BlockSpec((MOE_BLK, d), lambda i, be, nv: (i, 0)),
                  pl.BlockSpec((MOE_BLK, 1), lambda i, be, nv: (i, 0)),
                  pl.BlockSpec((1, d, de), lambda i, be, nv: (be[i], 0, 0)),
                  pl.BlockSpec((1, d, de), lambda i, be, nv: (be[i], 0, 0)),
                  pl.BlockSpec((1, de, d), lambda i, be, nv: (be[i], 0, 0))],
        out_specs=pl.BlockSpec((MOE_BLK, d), lambda i, be, nv: (i, 0)),
        scratch_shapes=[pltpu.VMEM((d, de), BF16), pltpu.VMEM((d, de), BF16), pltpu.VMEM((de, d), BF16)])
    return pl.pallas_call(
        _experts_kernel,
        out_shape=jax.ShapeDtypeStruct((p, d), F32),
        grid_spec=grid_spec,
        compiler_params=_cparams(("arbitrary",)),
        name="moe_experts",
    )(block_e, n_valid, xp, row_w, w1, w3, w2)


def _moe_final_kernel(x_ref, r_ref, w1_ref, w3_ref, w2_ref, lg_ref, lb_ref, o_ref, *, alpha):
    x = x_ref[...]
    xb = x.astype(BF16)
    h = _silu(_dot(xb, w1_ref[...])) * _dot(xb, w3_ref[...])
    shared = _dot(h.astype(BF16), w2_ref[...])
    o_ref[...] = _ln_rows(alpha * x + (r_ref[...] + shared), lg_ref[...], lb_ref[...])


def moe_final(x, routed, w1, w3, w2, lg, lb, alpha):
    t, d = x.shape
    tb = _tile(t, 512)
    ds = w1.shape[1]
    row = pl.BlockSpec((tb, d), lambda i: (i, 0))
    cst = lambda shape: pl.BlockSpec(shape, lambda i: tuple(0 for _ in shape))
    return pl.pallas_call(
        functools.partial(_moe_final_kernel, alpha=alpha),
        out_shape=jax.ShapeDtypeStruct((t, d), F32),
        grid=(t // tb,),
        in_specs=[row, row, cst((d, ds)), cst((d, ds)), cst((ds, d)), cst((1, d)), cst((1, d))],
        out_specs=row,
        compiler_params=_cparams(("parallel",)),
        name="moe_final",
    )(x, routed, w1.astype(BF16), w3.astype(BF16), w2.astype(BF16), lg.reshape(1, d), lb.reshape(1, d))


def moe_layer(x, lw, alpha):
    n, d = x.shape
    eidx_t, wsel_t = moe_router(x, lw['router_w'], lw['router_bias'])
    m = n * TOP_K
    flat_e = eidx_t.T.reshape(m)
    flat_w = wsel_t.T.reshape(m)
    nb = (m + N_EXPERTS * (MOE_BLK - 1) + MOE_BLK - 1) // MOE_BLK
    p = nb * MOE_BLK
    order = jnp.argsort(flat_e)
    se = flat_e[order]
    counts = jnp.bincount(flat_e, length=N_EXPERTS)
    pcounts = (counts + MOE_BLK - 1) // MOE_BLK * MOE_BLK
    starts = jnp.cumsum(counts) - counts
    pends = jnp.cumsum(pcounts)
    pstarts = pends - pcounts
    dest_sorted = pstarts[se] + (jnp.arange(m) - starts[se])
    row_tok = jnp.zeros((p,), I32).at[dest_sorted].set((order // TOP_K).astype(I32))
    row_w = jnp.zeros((p,), F32).at[dest_sorted].set(flat_w[order])
    dest = jnp.zeros((m,), I32).at[order].set(dest_sorted.astype(I32))
    block_e = jnp.minimum(jnp.searchsorted(pends, jnp.arange(nb) * MOE_BLK, side='right'), N_EXPERTS - 1).astype(I32)
    n_valid = (pends[-1] // MOE_BLK).astype(I32).reshape(1)
    xp = x.astype(BF16)[row_tok]
    yp = moe_experts(xp, row_w.reshape(p, 1), block_e, n_valid, lw['exp_w1'], lw['exp_w3'], lw['exp_w2'])
    routed = jnp.sum(yp[dest.reshape(n, TOP_K)], axis=1)
    return moe_final(x, routed, lw['sh_w1'], lw['sh_w3'], lw['sh_w2'], lw['ln2_g'], lw['ln2_b'], alpha)


def _rope_rows(x, pos):
    half = x.shape[-1] // 2
    inv = ROPE_THETA ** (-jnp.arange(half, dtype=F32) / half)
    ang = pos.astype(F32)[:, None] * inv
    cos = jnp.cos(ang)[:, None, :]
    sin = jnp.sin(ang)[:, None, :]
    x1, x2 = x[..., :half], x[..., half:]
    return jnp.concatenate([x1 * cos - x2 * sin, x2 * cos + x1 * sin], -1)


def _sample_mixers(zs, pos, lw, mem_k, mem_v, gla_s0, conv0, h0, ck, cv, cki, page_table):
    b, t = zs.shape[:2]
    seg = lambda off, w: zs[..., off:off + w]
    q = seg(Z_GQ, 512).reshape(b, t, GLA_HEADS, GLA_DK) * (GLA_DK ** -0.5)
    k = seg(Z_GK, 512).reshape(b, t, GLA_HEADS, GLA_DK)
    v = seg(Z_GV, 1024).reshape(b, t, GLA_HEADS, GLA_DV)
    gr = seg(Z_GR, 1024)
    glr = seg(Z_GLR, GLA_RANK)
    glog = jax.nn.log_sigmoid(glr @ lw['gla_w_gate2'] + lw['gla_b_gate']) / GLA_TAU
    glog = glog.reshape(b, t, GLA_HEADS, GLA_DK)
    s = gla_s0.astype(F32)
    outs = []
    for i in range(t):
        s = jnp.exp(glog[:, i])[..., None] * s + k[:, i][..., None] * v[:, i][:, :, None, :]
        outs.append(jnp.einsum('bhk,bhkv->bhv', q[:, i], s))
    o = jnp.stack(outs, 1)
    on = o * lax.rsqrt(jnp.mean(jnp.square(o), -1, keepdims=True) + LN_EPS) * lw['gla_norm_g']
    o_gla = on.reshape(b, t, -1) * jax.nn.silu(gr)
    s_new = s
    qd = _rope_rows(seg(Z_DQ, 1024).reshape(b, t, DSA_HEADS, DSA_HD), pos)
    kd = _rope_rows(seg(Z_DK, 512).reshape(b, t, DSA_KV_HEADS, DSA_HD), pos)
    vd = seg(Z_DV, 512).reshape(b, t, DSA_KV_HEADS, DSA_HD)
    qi = _rope_rows(seg(Z_IQ, 512).reshape(b, t, IDX_HEADS, IDX_D), pos)
    ik = seg(Z_MISC, IDX_D)
    mu = jnp.mean(ik, -1, keepdims=True)
    var = jnp.mean(jnp.square(ik - mu), -1, keepdims=True)
    ikn = (ik - mu) * lax.rsqrt(var + LN_EPS) * lw['idx_ln_g'] + lw['idx_ln_b']
    ki = _rope_rows(ikn[:, :, None], pos)[:, :, 0]
    wi = zs[..., Z_MISC + IDX_D:Z_MISC + IDX_D + IDX_HEADS] * (IDX_HEADS ** -0.5)
    past = page_table.shape[1] * PAGE_SIZE
    k_sel = min(TOPK_MAX, (past + t) // 4)
    ki_past = cki[page_table].reshape(b, past, IDX_D)
    kidx = jnp.concatenate([ki_past, ki], axis=1)
    sc = jnp.einsum('bthd,bsd->bths', qi, kidx) * (IDX_D ** -0.5)
    score = jnp.einsum('bths,bth->bts', jax.nn.relu(sc), wi)
    adm = jnp.arange(past + t)[None, :] <= pos[:, None]
    score = jnp.where(adm[None], score, -jnp.inf)
    _, sel = lax.top_k(score, k_sel)
    valid = sel <= pos[None, :, None]
    take = lambda a, idx: jax.vmap(lambda r, ss: r[ss])(a, idx)
    sp = jnp.minimum(sel, past - 1)
    phys = take(page_table, sp // PAGE_SIZE)
    slot = sp % PAGE_SIZE
    sn = jnp.clip(sel - past, 0, t - 1)
    is_new = (sel >= past)[..., None, None]
    kg = jnp.where(is_new, take(kd, sn), ck[phys, slot])
    vg = jnp.where(is_new, take(vd, sn), cv[phys, slot])
    qg = qd.reshape(b, t, DSA_KV_HEADS, DSA_HEADS // DSA_KV_HEADS, DSA_HD)
    s_att = jnp.einsum('btngd,btsnd->btngs', qg, kg) * (DSA_HD ** -0.5)
    s_att = jnp.where(valid[:, :, None, None, :], s_att, -jnp.inf)
    p = jax.nn.softmax(s_att, axis=-1)
    o_dsa = jnp.einsum('btngs,btsnd->btngd', p, vg).reshape(b, t, -1)
    lx, ly = seg(Z_LX, LRU_W), seg(Z_LY, LRU_W)
    xp = jnp.concatenate([conv0, lx], axis=1)
    xc = lw['conv_b']
    for i in range(CONV_W):
        xc = xc + xp[:, i:i + t] * lw['conv_w'][i]
    conv_new = xp[:, xp.shape[1] - (CONV_W - 1):]
    xblk = xc.reshape(b, t, LRU_BLOCKS, LRU_BW)
    gate_r = jax.nn.sigmoid(jnp.einsum('btnj,njk->btnk', xblk, lw['lru_wa']).reshape(b, t, LRU_W) + lw['lru_ba'])
    gate_i = jax.nn.sigmoid(jnp.einsum('btnj,njk->btnk', xblk, lw['lru_wx']).reshape(b, t, LRU_W) + lw['lru_bx'])
    log_a = -LRU_C * gate_r * jax.nn.softplus(-lw['lru_lambda'])
    a = jnp.exp(log_a)
    mult = jnp.sqrt(-jnp.expm1(2.0 * log_a))
    mult = jnp.where((pos == 0)[None, :, None], 1.0, mult)
    bt = mult * gate_i * xc
    h = h0.astype(F32)
    hs = []
    for i in range(t):
        h = a[:, i] * h + bt[:, i]
        hs.append(h)
    o_lru = jnp.stack(hs, 1) * jax.nn.gelu(ly)
    mq = seg(Z_MQ, 1024).reshape(b, t, MEM_HEADS, MEM_HD)
    sm = jnp.einsum('bthd,bmhd->bhtm', mq, mem_k) * (MEM_HD ** -0.5)
    pm = jax.nn.softmax(sm, axis=-1)
    o_mem = jnp.einsum('bhtm,bmhd->bthd', pm, mem_v).reshape(b, t, -1)
    return (o_gla, o_dsa, o_lru, o_mem), (kd, vd, ki, s_new, conv_new, h)


def _pack_w_in(w):
    parts, start = [], 0
    for s in IN_SIZES:
        parts.append(w[:, start:start + s])
        start += s
    gq, gk, gv, gr, glr, dq, dk, dv, iq, ik, iw, lx, ly, mq, gl = parts
    d = w.shape[0]
    padc = lambda a, n: jnp.pad(a, ((0, 0), (0, n - a.shape[1])))
    cols = [gq, gk, gv, gr, dq, dk, dv, iq, padc(glr, LANE), padc(jnp.concatenate([ik, iw], 1), LANE),
            jnp.zeros((d, Z_LX - Z_MISC - LANE), w.dtype), lx, ly, mq, gl]
    out = jnp.concatenate(cols, axis=1).astype(BF16)
    assert out.shape[1] == Z_W
    return out


def kernel(x_prompt, x_sample, cache_k, cache_v, cache_idx_k, cache_mem_k, cache_mem_v, state_gla, state_conv, state_lru, page_table, mem_prompt, ln_in_g, ln_in_b, w_in, b_gate, gla_w_gate2, gla_b_gate, gla_norm_g, idx_ln_g, idx_ln_b, conv_w, conv_b, lru_wa, lru_ba, lru_wx, lru_bx, lru_lambda, mem_w_kv, w_branch, w_out, ln1_g, ln1_b, ln2_g, ln2_b, router_w, router_bias, exp_w1, exp_w3, exp_w2, sh_w1, sh_w3, sh_w2):
    bp, sp_len, d = x_prompt.shape
    assert bp == 1, "prompt group is a single sequence"
    bs, ts, _ = x_sample.shape
    depth = w_in.shape[0]
    alpha = (2 * depth) ** 0.25
    past = page_table.shape[1] * PAGE_SIZE
    pos_p = jnp.arange(sp_len, dtype=I32)
    pos_s = past + jnp.arange(ts, dtype=I32)
    tabs_p = rope_tables(pos_p)

    hp = layer_norm_rows(x_prompt.reshape(sp_len, d), ln_in_g, ln_in_b)
    hs = layer_norm_rows(x_sample.reshape(bs * ts, d), ln_in_g, ln_in_b)

    st_p, st_s, mem_new = [], [], []
    for l in range(depth):
        lw = {'gla_w_gate2': gla_w_gate2[l], 'gla_b_gate': gla_b_gate[l], 'gla_norm_g': gla_norm_g[l],
              'idx_ln_g': idx_ln_g[l], 'idx_ln_b': idx_ln_b[l],
              'conv_w': conv_w[l], 'conv_b': conv_b[l], 'lru_wa': lru_wa[l], 'lru_ba': lru_ba[l],
              'lru_wx': lru_wx[l], 'lru_bx': lru_bx[l], 'lru_lambda': lru_lambda[l],
              'ln2_g': ln2_g[l], 'ln2_b': ln2_b[l], 'router_w': router_w[l], 'router_bias': router_bias[l],
              'exp_w1': exp_w1[l], 'exp_w3': exp_w3[l], 'exp_w2': exp_w2[l],
              'sh_w1': sh_w1[l], 'sh_w3': sh_w3[l], 'sh_w2': sh_w2[l]}
        w_in_p = _pack_w_in(w_in[l])
        wb = w_branch[l].astype(BF16)
        wo = w_out[l].astype(BF16)

        zp = matmul(hp, w_in_p)
        mkv = matmul(mem_prompt.reshape(N_MEM, d), mem_w_kv[l].astype(BF16))
        q_bf, kd, k_bf, vd, v_bf, qi_bf, ki, ki_bf, wi = dsa_prep(zp, tabs_p, idx_ln_g[l], idx_ln_b[l])
        o_gla, s_gla = gla_prompt(zp, gla_w_gate2[l], gla_b_gate[l], gla_norm_g[l])
        o_dsa = dsa_prompt(q_bf, qi_bf, wi, ki_bf.T, k_bf.T, v_bf)
        o_lru, h_last = rglru_prompt(zp, conv_w[l], conv_b[l], lru_wa[l], lru_ba[l], lru_wx[l], lru_bx[l], lru_lambda[l])
        o_mem = mem_attend_prompt(zp, mkv)
        xp1 = merge_out((o_gla, o_dsa, o_lru, o_mem), zp, hp, wb, b_gate[l], wo, ln1_g[l], ln1_b[l], alpha)
        lx_p = zp[:, Z_LX:Z_LX + LRU_W]
        conv_p = jnp.concatenate([jnp.zeros((CONV_W - 1, LRU_W), F32), lx_p], 0)[-(CONV_W - 1):]
        st_p.append((kd.reshape(1, sp_len, DSA_KV_HEADS, DSA_HD), vd.reshape(1, sp_len, DSA_KV_HEADS, DSA_HD),
                     ki.reshape(1, sp_len, IDX_D), s_gla[None], conv_p[None], h_last.reshape(1, LRU_W)))
        mem_new.append((mkv[:, :MEM_HEADS * MEM_HD].reshape(1, N_MEM, MEM_HEADS, MEM_HD),
                        mkv[:, MEM_HEADS * MEM_HD:].reshape(1, N_MEM, MEM_HEADS, MEM_HD)))

        zs = matmul(hs, w_in_p)
        brs, sts = _sample_mixers(zs.reshape(bs, ts, Z_W), pos_s, lw, cache_mem_k[l], cache_mem_v[l],
                                  state_gla[l], state_conv[l], state_lru[l],
                                  cache_k[l], cache_v[l], cache_idx_k[l], page_table)
        brs = tuple(a.reshape(bs * ts, d) for a in brs)
        xs1 = merge_out(brs, zs, hs, wb, b_gate[l], wo, ln1_g[l], ln1_b[l], alpha)
        st_s.append(sts)

        x_all = moe_layer(jnp.concatenate([xp1, xs1], 0), lw, alpha)
        hp, hs = x_all[:sp_len], x_all[sp_len:]

    stk = lambda seq, j: jnp.stack([t[j] for t in seq], axis=0)
    k_p, v_p, ik_p, gla_p, conv_p_, lru_p = [stk(st_p, j) for j in range(6)]
    k_s, v_s, ik_s, gla_s, conv_s, lru_s = [stk(st_s, j) for j in range(6)]
    memk_p, memv_p = stk(mem_new, 0), stk(mem_new, 1)
    return (hp.reshape(1, sp_len, d), hs.reshape(bs, ts, d), k_p, v_p, ik_p, gla_p, conv_p_, lru_p, memk_p, memv_p,
            k_s, v_s, ik_s, gla_s, conv_s, lru_s)
```

```python
import functools
import math

import jax
import jax.numpy as jnp
from jax import lax
from jax.experimental import pallas as pl
from jax.experimental.pallas import tpu as pltpu

F32 = jnp.float32
BF16 = jnp.bfloat16
I32 = jnp.int32

D_MODEL = 1024
PAGE_SIZE = 128
N_BRANCH = 4
GLA_HEADS = 4
GLA_DK = 128
GLA_DV = 256
GLA_RANK = 16
GLA_TAU = 16.0
GLA_CHUNK = 64
DSA_HEADS = 8
DSA_KV_HEADS = 4
DSA_HD = 128
DSA_G = DSA_HEADS // DSA_KV_HEADS
IDX_HEADS = 8
IDX_D = 64
TOPK_MAX = 256
LRU_W = 1024
LRU_BLOCKS = 4
LRU_BW = LRU_W // LRU_BLOCKS
CONV_W = 4
LRU_C = 8.0
N_MEM = 256
MEM_HEADS = 4
MEM_HD = 256
N_EXPERTS = 256
TOP_K = 8
N_GROUPS = 8
TOPK_GROUPS = 4
ROUTED_SCALE = 2.5
ROPE_THETA = 10000.0
LN_EPS = 1e-5

IN_SIZES = (GLA_HEADS * GLA_DK, GLA_HEADS * GLA_DK, GLA_HEADS * GLA_DV, GLA_HEADS * GLA_DV, GLA_RANK,
            DSA_HEADS * DSA_HD, DSA_KV_HEADS * DSA_HD, DSA_KV_HEADS * DSA_HD,
            IDX_HEADS * IDX_D, IDX_D, IDX_HEADS,
            LRU_W, LRU_W, MEM_HEADS * MEM_HD, N_BRANCH * D_MODEL)

LANE = 128
SUB = 8
Z_GQ, Z_GK, Z_GV, Z_GR = 0, 512, 1024, 2048
Z_DQ, Z_DK, Z_DV, Z_IQ = 3072, 4096, 4608, 5120
Z_GLR, Z_MISC = 5632, 5760
Z_LX, Z_LY, Z_MQ, Z_GL = 6144, 7168, 8192, 9216
Z_W = 13312

VMEM_LIMIT = 56 * 1024 * 1024
MOE_BLK = 256
INT_MIN = -2 ** 31
NEG_BIG = -1e30
LOG2E = 1.4426950408889634
SQ_ROWS = 2 * SUB


def _tile(n, target):
    if n <= target:
        return n
    for t in range(target, 7, -1):
        if n % t == 0 and t % 8 == 0:
            return t
    return n


def _cparams(sem, vmem=None):
    return pltpu.CompilerParams(dimension_semantics=sem, vmem_limit_bytes=vmem or VMEM_LIMIT)


def _ln_rows(x, g, b):
    mu = jnp.mean(x, -1, keepdims=True)
    xc = x - mu
    var = jnp.mean(xc * xc, -1, keepdims=True)
    return xc * lax.rsqrt(var + LN_EPS) * g + b


def _sigmoid(x):
    return 1.0 / (1.0 + jnp.exp(-x))


def _silu(x):
    return x * _sigmoid(x)


def _dot(a, b):
    return jnp.dot(a, b, preferred_element_type=F32)


def _dot_nt(a, b):
    return lax.dot_general(a, b, (((1,), (1,)), ((), ())), preferred_element_type=F32)


def _dot_tn(a, b):
    return lax.dot_general(a, b, (((0,), (0,)), ((), ())), preferred_element_type=F32)


def _cst(shape):
    return pl.BlockSpec(shape, lambda *_: tuple(0 for _ in shape))


def _ln_kernel(x_ref, g_ref, b_ref, o_ref):
    o_ref[...] = _ln_rows(x_ref[...], g_ref[...], b_ref[...])


def layer_norm_rows(x, g, b):
    m, d = x.shape
    tm = _tile(m, 512)
    return pl.pallas_call(
        _ln_kernel,
        out_shape=jax.ShapeDtypeStruct((m, d), F32),
        grid=(m // tm,),
        in_specs=[pl.BlockSpec((tm, d), lambda i: (i, 0)), _cst((1, d)), _cst((1, d))],
        out_specs=pl.BlockSpec((tm, d), lambda i: (i, 0)),
        compiler_params=_cparams(("parallel",)),
        name="ln_rows",
    )(x, g.reshape(1, d), b.reshape(1, d))


def _mm_kernel(x_ref, w_ref, o_ref, xb_ref):
    @pl.when(pl.program_id(1) == 0)
    def _():
        xb_ref[...] = x_ref[...].astype(BF16)

    o_ref[...] = _dot(xb_ref[...], w_ref[...])


def matmul(x, w):
    m, k = x.shape
    n = w.shape[1]
    tm = _tile(m, 1024)
    tn = _tile(n, 1024)
    return pl.pallas_call(
        _mm_kernel,
        out_shape=jax.ShapeDtypeStruct((m, n), F32),
        grid=(m // tm, n // tn),
        in_specs=[pl.BlockSpec((tm, k), lambda i, j: (i, 0)),
                  pl.BlockSpec((k, tn), lambda i, j: (0, j))],
        out_specs=pl.BlockSpec((tm, tn), lambda i, j: (i, j)),
        scratch_shapes=[pltpu.VMEM((tm, k), BF16)],
        compiler_params=_cparams(("parallel", "arbitrary")),
        name="matmul",
    )(x, w)


def _rot_half(x, width):
    if width == LANE:
        return pltpu.roll(x, LANE // 2, 1)
    half = width // 2
    lane = lax.broadcasted_iota(I32, x.shape, 1)
    first = (lane % width) < half
    return jnp.where(first, pltpu.roll(x, LANE - half, 1), pltpu.roll(x, half, 1))


def _prep_kernel(dq_ref, dk_ref, dv_ref, iq_ref, misc_ref, c128_ref, s128_ref, c64_ref, s64_ref,
                 ig_ref, ib_ref,
                 q_ref, k_ref, kb_ref, v_ref, vb_ref, qi_ref, ki_ref, kib_ref, wi_ref):
    c128, s128 = c128_ref[...], s128_ref[...]
    c64, s64 = c64_ref[...], s64_ref[...]
    qscale = (DSA_HD ** -0.5) * LOG2E
    for h in range(DSA_HEADS):
        sl = slice(h * LANE, (h + 1) * LANE)
        x = dq_ref[:, sl]
        q_ref[:, sl] = ((x * c128 + _rot_half(x, LANE) * s128) * qscale).astype(BF16)
    for h in range(DSA_KV_HEADS):
        sl = slice(h * LANE, (h + 1) * LANE)
        x = dk_ref[:, sl]
        r = x * c128 + _rot_half(x, LANE) * s128
        k_ref[:, sl] = r
        kb_ref[:, sl] = r.astype(BF16)
    v = dv_ref[...]
    v_ref[...] = v
    vb_ref[...] = v.astype(BF16)
    for c in range(IDX_HEADS * IDX_D // LANE):
        sl = slice(c * LANE, (c + 1) * LANE)
        x = iq_ref[:, sl]
        qi_ref[:, sl] = (x * c64 + _rot_half(x, IDX_D) * s64).astype(BF16)
    misc = misc_ref[...]
    lane = lax.broadcasted_iota(I32, misc.shape, 1)
    isk = lane < IDX_D
    mu = jnp.sum(jnp.where(isk, misc, 0.0), -1, keepdims=True) * (1.0 / IDX_D)
    xc = jnp.where(isk, misc - mu, 0.0)
    var = jnp.sum(xc * xc, -1, keepdims=True) * (1.0 / IDX_D)
    kn = xc * lax.rsqrt(var + LN_EPS) * ig_ref[...] + ib_ref[...]
    kr = kn * c64 + _rot_half(kn, IDX_D) * s64
    ki_ref[...] = kr[:, :IDX_D]
    kib_ref[...] = kr[:, :IDX_D].astype(BF16)
    w = pltpu.roll(misc, LANE - IDX_D, 1)
    wi_ref[...] = jnp.where(lane < IDX_HEADS, w * (IDX_HEADS ** -0.5) * (IDX_D ** -0.5), 0.0)


def dsa_prep(z, tabs, idx_g, idx_b):
    t = z.shape[0]
    tb = _tile(t, 512)
    c128, s128, c64, s64 = tabs
    zspec = lambda w, off: pl.BlockSpec((tb, w), lambda i, o=off // w: (i, o))
    row = lambda w: pl.BlockSpec((tb, w), lambda i: (i, 0))
    pad = lambda a: jnp.pad(a, (0, LANE - IDX_D)).reshape(1, LANE)
    return pl.pallas_call(
        _prep_kernel,
        out_shape=[jax.ShapeDtypeStruct((t, 1024), BF16),
                   jax.ShapeDtypeStruct((t, 512), F32),
                   jax.ShapeDtypeStruct((t, 512), BF16),
                   jax.ShapeDtypeStruct((t, 512), F32),
                   jax.ShapeDtypeStruct((t, 512), BF16),
                   jax.ShapeDtypeStruct((t, 512), BF16),
                   jax.ShapeDtypeStruct((t, IDX_D), F32),
                   jax.ShapeDtypeStruct((t, IDX_D), BF16),
                   jax.ShapeDtypeStruct((t, LANE), F32)],
        grid=(t // tb,),
        in_specs=[zspec(1024, Z_DQ), zspec(512, Z_DK), zspec(512, Z_DV), zspec(512, Z_IQ), zspec(LANE, Z_MISC),
                  row(LANE), row(LANE), row(LANE), row(LANE), _cst((1, LANE)), _cst((1, LANE))],
        out_specs=[row(1024), row(512), row(512), row(512), row(512), row(512), row(IDX_D), row(IDX_D), row(LANE)],
        compiler_params=_cparams(("parallel",)),
        name="dsa_prep",
    )(z, z, z, z, z, c128, s128, c64, s64, pad(idx_g), pad(idx_b))


def rope_tables(pos):
    posf = pos.astype(F32)[:, None]

    def tab(width):
        half = width // 2
        inv = ROPE_THETA ** (-jnp.arange(half, dtype=F32) / half)
        ang = posf * inv
        c, s = jnp.cos(ang), jnp.sin(ang)
        reps = LANE // width
        return jnp.tile(jnp.concatenate([c, c], 1), (1, reps)), jnp.tile(jnp.concatenate([-s, s], 1), (1, reps))

    c128, s128 = tab(DSA_HD)
    c64, s64 = tab(IDX_D)
    return c128, s128, c64, s64


def _log_sigmoid(x):
    return jnp.minimum(x, 0.0) - jnp.log1p(jnp.exp(-jnp.abs(x)))


def _gla_kernel(q_ref, k_ref, v_ref, r_ref, glr_ref, w2_ref, bg_ref, ng_ref, o_ref, st_ref, s_ref, *, tb):
    c = GLA_CHUNK
    j = pl.program_id(1)

    @pl.when(j == 0)
    def _():
        s_ref[...] = jnp.zeros_like(s_ref)

    ri = lax.broadcasted_iota(I32, (c, c), 0)
    ci = lax.broadcasted_iota(I32, (c, c), 1)
    causal = ci <= ri
    tri = causal.astype(F32)
    w2 = w2_ref[...]
    bg = bg_ref[...]
    ng = ng_ref[...]

    def chunk(ic, carry):
        r0 = pl.multiple_of(ic * c, c)
        q = q_ref[pl.ds(r0, c), :] * (GLA_DK ** -0.5)
        k = k_ref[pl.ds(r0, c), :]
        v = v_ref[pl.ds(r0, c), :].astype(BF16)
        x = _dot(glr_ref[pl.ds(r0, c), :].astype(BF16), w2) + bg
        g = _log_sigmoid(x) * (1.0 / GLA_TAU)
        b = jnp.dot(tri, g, preferred_element_type=F32, precision=lax.Precision.HIGHEST)
        bm = b[c // 2 - 1:c // 2, :]
        bl = b[c - 1:c, :]
        st = s_ref[...]
        qe = (q * jnp.exp(b - bm)).astype(BF16)
        ke = (k * jnp.exp(bm - b)).astype(BF16)
        a = jnp.where(causal, _dot_nt(qe, ke), 0.0)
        o = _dot_nt((q * jnp.exp(b)).astype(BF16), st.astype(BF16)) + _dot(a.astype(BF16), v)
        kd = (k * jnp.exp(bl - b)).astype(BF16)
        s_ref[...] = st * jnp.exp(bl) + _dot_tn(v, kd)
        on = o * lax.rsqrt(jnp.mean(o * o, -1, keepdims=True) + LN_EPS) * ng
        o_ref[pl.ds(r0, c), :] = on * _silu(r_ref[pl.ds(r0, c), :])
        return carry

    lax.fori_loop(0, tb // c, chunk, 0)

    @pl.when(j == pl.num_programs(1) - 1)
    def _():
        st_ref[0] = s_ref[...]


def gla_prompt(z, w2, bg, ng):
    t = z.shape[0]
    tb = _tile(t, 512)
    assert tb % GLA_CHUNK == 0
    h = GLA_HEADS
    w2p = jnp.pad(w2, ((0, LANE - GLA_RANK), (0, 0))).astype(BF16)
    o, st = pl.pallas_call(
        functools.partial(_gla_kernel, tb=tb),
        out_shape=[jax.ShapeDtypeStruct((t, h * GLA_DV), F32),
                   jax.ShapeDtypeStruct((h, GLA_DV, GLA_DK), F32)],
        grid=(h, t // tb),
        in_specs=[pl.BlockSpec((tb, GLA_DK), lambda hh, j: (j, Z_GQ // GLA_DK + hh)),
                  pl.BlockSpec((tb, GLA_DK), lambda hh, j: (j, Z_GK // GLA_DK + hh)),
                  pl.BlockSpec((tb, GLA_DV), lambda hh, j: (j, Z_GV // GLA_DV + hh)),
                  pl.BlockSpec((tb, GLA_DV), lambda hh, j: (j, Z_GR // GLA_DV + hh)),
                  pl.BlockSpec((tb, LANE), lambda hh, j: (j, Z_GLR // LANE)),
                  pl.BlockSpec((LANE, GLA_DK), lambda hh, j: (0, hh)),
                  pl.BlockSpec((1, GLA_DK), lambda hh, j: (0, hh)),
                  pl.BlockSpec((1, GLA_DV), lambda hh, j: (0, 0))],
        out_specs=[pl.BlockSpec((tb, GLA_DV), lambda hh, j: (j, hh)),
                   pl.BlockSpec((1, GLA_DV, GLA_DK), lambda hh, j: (hh, 0, 0))],
        scratch_shapes=[pltpu.VMEM((GLA_DV, GLA_DK), F32)],
        compiler_params=_cparams(("parallel", "arbitrary")),
        name="gla_prompt",
    )(z, z, z, z, z, w2p, bg.reshape(1, -1), ng.reshape(1, -1))
    return o, jnp.swapaxes(st, 1, 2)


def _sort_key(x):
    bits = lax.bitcast_convert_type(x + 0.0, I32)
    return bits ^ ((bits >> 31) & 0x7FFFFFFF)


def _kth_largest_key(count_ge, k_sel):
    kf = float(k_sel)
    lo = jnp.where(count_ge(0) >= kf, 0, INT_MIN).astype(I32)

    def bit_step(it, lo):
        cand = lo + jnp.left_shift(jnp.int32(1), 30 - it)
        return jnp.where(count_ge(cand) >= kf, cand, lo)

    tau = lax.fori_loop(0, 31, bit_step, lo)
    n_ge = count_ge(tau)
    n_gt = count_ge(tau + 1)
    need = kf - n_gt
    tie = jnp.where(tau > INT_MIN, n_ge - n_gt - need, 0.0)
    return tau, need, tie


def _dsa_kernel(q_ref, qi_ref, wi_ref, kit_ref, kt_ref, v_ref, o_ref,
                key_ref, tri_ref, q2_ref, m_ref, l_ref, acc_ref, *, tq, tk, ts, k_sel):
    i = pl.program_id(0)

    @pl.when(i == 0)
    def _():
        r = lax.broadcasted_iota(I32, (tk, tk), 0)
        c = lax.broadcasted_iota(I32, (tk, tk), 1)
        tri_ref[...] = (r <= c).astype(BF16)

    t0 = i * tq
    n_kt = (t0 + tq + tk - 1) // tk
    wi = wi_ref[...]
    for n in range(DSA_KV_HEADS):
        for g in range(DSA_G):
            h = n * DSA_G + g
            q2_ref[n, g * tq:(g + 1) * tq, :] = q_ref[:, h * DSA_HD:(h + 1) * DSA_HD]

    row_s = t0 + lax.broadcasted_iota(I32, (tq, ts), 0)

    def score_chunk(kc, carry):
        c0 = pl.multiple_of(kc * ts, ts)
        acc = jnp.zeros((tq, ts), F32)
        for h in range(IDX_HEADS):
            sc = _dot(qi_ref[:, h * IDX_D:(h + 1) * IDX_D], kit_ref[:, pl.ds(c0, ts)])
            acc = acc + jnp.maximum(sc, 0.0) * wi[:, h:h + 1]
        col = c0 + lax.broadcasted_iota(I32, (tq, ts), 1)
        key_ref[:, pl.ds(c0, ts)] = jnp.where(col <= row_s, _sort_key(acc), INT_MIN)
        return carry

    lax.fori_loop(0, n_kt * (tk // ts), score_chunk, 0)

    nl = tk // LANE

    def count_ge(cand):
        cb = jnp.broadcast_to(cand, (tq, LANE))

        def body(kt, acc):
            c0 = pl.multiple_of(kt * tk, tk)
            for j in range(nl):
                acc = acc + jnp.where(key_ref[:, pl.ds(c0 + j * LANE, LANE)] >= cb, 1, 0)
            return acc

        acc = lax.fori_loop(0, n_kt, body, jnp.zeros((tq, LANE), I32))
        return jnp.sum(acc.astype(F32), axis=1, keepdims=True)

    tau, need, tie = _kth_largest_key(count_ge, k_sel)
    tau_eff = jnp.maximum(tau, INT_MIN + 1)
    has_tie = jnp.max(tie) > 0.0

    m_ref[...] = jnp.full_like(m_ref, NEG_BIG)
    l_ref[...] = jnp.zeros_like(l_ref)
    acc_ref[...] = jnp.zeros_like(acc_ref)

    def attend(kt, run_eq):
        c0 = pl.multiple_of(kt * tk, tk)
        keys = key_ref[:, pl.ds(c0, tk)]

        def fast(run_eq):
            return jnp.where(keys >= tau_eff, 0.0, NEG_BIG), run_eq

        def slow(run_eq):
            eq = (keys == tau) & (keys > INT_MIN)
            rank = run_eq + _dot(eq.astype(BF16), tri_ref[...])
            sel = (keys > tau) | (eq & (rank <= need))
            return jnp.where(sel, 0.0, NEG_BIG), run_eq + jnp.sum(eq.astype(F32), axis=1, keepdims=True)

        bias, run_eq = lax.cond(has_tie, slow, fast, run_eq)
        bias2 = jnp.concatenate([bias] * DSA_G, axis=0)
        for n in range(DSA_KV_HEADS):
            ktile = kt_ref[n * DSA_HD:(n + 1) * DSA_HD, pl.ds(c0, tk)]
            vtile = v_ref[pl.ds(c0, tk), n * DSA_HD:(n + 1) * DSA_HD]
            s = _dot(q2_ref[n], ktile) + bias2
            m_old = m_ref[n]
            m_new = jnp.maximum(m_old, jnp.max(s, axis=1, keepdims=True))
            p = jnp.exp2(s - m_new)
            alpha = jnp.exp2(m_old - m_new)
            l_ref[n] = alpha * l_ref[n] + jnp.sum(p, axis=1, keepdims=True)
            acc_ref[n] = alpha * acc_ref[n] + _dot(p.astype(BF16), vtile)
            m_ref[n] = m_new
        return run_eq

    lax.fori_loop(0, n_kt, attend, jnp.zeros((tq, 1), F32))
    for n in range(DSA_KV_HEADS):
        o = acc_ref[n] / l_ref[n]
        for g in range(DSA_G):
            h = n * DSA_G + g
            o_ref[:, h * DSA_HD:(h + 1) * DSA_HD] = o[g * tq:(g + 1) * tq, :]


def dsa_prompt(q_bf, qi_bf, wi, kit_bf, kt_bf, v_bf):
    t = q_bf.shape[0]
    tq = _tile(t, 128)
    tk = _tile(t, 1024)
    ts = _tile(tk, 256)
    k_sel = min(TOPK_MAX, t // 4)
    whole = lambda shape: pl.BlockSpec(shape, lambda i: (0, 0), pipeline_mode=pl.Buffered(1))
    return pl.pallas_call(
        functools.partial(_dsa_kernel, tq=tq, tk=tk, ts=ts, k_sel=k_sel),
        out_shape=jax.ShapeDtypeStruct((t, DSA_HEADS * DSA_HD), F32),
        grid=(t // tq,),
        in_specs=[pl.BlockSpec((tq, DSA_HEADS * DSA_HD), lambda i: (i, 0)),
                  pl.BlockSpec((tq, IDX_HEADS * IDX_D), lambda i: (i, 0)),
                  pl.BlockSpec((tq, LANE), lambda i: (i, 0)),
                  whole((IDX_D, t)), whole((DSA_KV_HEADS * DSA_HD, t)), whole((t, DSA_KV_HEADS * DSA_HD))],
        out_specs=pl.BlockSpec((tq, DSA_HEADS * DSA_HD), lambda i: (i, 0)),
        scratch_shapes=[pltpu.VMEM((tq, t), I32),
                        pltpu.VMEM((tk, tk), BF16),
                        pltpu.VMEM((DSA_KV_HEADS, DSA_G * tq, DSA_HD), BF16),
                        pltpu.VMEM((DSA_KV_HEADS, DSA_G * tq, 1), F32),
                        pltpu.VMEM((DSA_KV_HEADS, DSA_G * tq, 1), F32),
                        pltpu.VMEM((DSA_KV_HEADS, DSA_G * tq, DSA_HD), F32)],
        compiler_params=_cparams(("arbitrary",)),
        name="dsa_prompt",
    )(q_bf, qi_bf, wi, kit_bf, kt_bf, v_bf)


def _sdsa_select_kernel(pt_ref, qi_ref, w_ref, kn_ref, *rest, ppc, ts, k_sel, past):
    pages = rest[:ppc]
    keys_out, tau_out, need_out, tie_out = rest[ppc:ppc + 4]
    key_ref, kc_ref = rest[ppc + 4:]
    c = pl.program_id(1)
    nch = pl.num_programs(1)
    qi = qi_ref[0]
    w = w_ref[0]

    def token_scores(sc):
        x = jnp.maximum(sc, 0.0) * w
        parts = [jnp.sum(x[t * IDX_HEADS:(t + 1) * IDX_HEADS], axis=0, keepdims=True) for t in range(ts)]
        return jnp.concatenate(parts + [jnp.zeros((SUB - ts, sc.shape[1]), F32)], axis=0)

    for j in range(ppc):
        kc_ref[j * PAGE_SIZE:(j + 1) * PAGE_SIZE, :] = pages[j][0, 0].astype(BF16)
    width = ppc * PAGE_SIZE
    score = token_scores(_dot_nt(qi, kc_ref[...]))
    rows = lax.broadcasted_iota(I32, (SUB, width), 0)
    key_ref[:, pl.ds(pl.multiple_of(c * width, width), width)] = jnp.where(rows < ts, _sort_key(score), INT_MIN)

    @pl.when(c == nch - 1)
    def _():
        kn = jnp.concatenate([kn_ref[0], jnp.zeros((LANE - ts, IDX_D), F32)], axis=0).astype(BF16)
        sc = token_scores(_dot_nt(qi, kn))
        r8 = lax.broadcasted_iota(I32, (SUB, LANE), 0)
        l8 = lax.broadcasted_iota(I32, (SUB, LANE), 1)
        ok = (r8 < ts) & (l8 <= r8)
        key_ref[:, past:past + LANE] = jnp.where(ok, _sort_key(sc), INT_MIN)

        def count_ge(cand):
            return jnp.sum(jnp.where(key_ref[...] >= cand, 1.0, 0.0), axis=1, keepdims=True)

        tau, need, tie = _kth_largest_key(count_ge, k_sel)
        keys_out[0] = key_ref[...]
        tau_out[0] = jnp.broadcast_to(tau, (SUB, LANE))
        need_out[0] = jnp.broadcast_to(need, (SUB, LANE))
        tie_out[0] = jnp.broadcast_to(tie, (SUB, LANE))


def _sdsa_attend_kernel(pt_ref, q_ref, keys_ref, ktail_ref, tau_ref, need_ref, tie_ref, kn_ref, vn_ref, *rest,
                        ppc, ts):
    kpages = rest[:ppc]
    vpages = rest[ppc:2 * ppc]
    o_ref = rest[2 * ppc]
    m_ref, l_ref, acc_ref, run_ref, tri_ref, kc_ref, vc_ref = rest[2 * ppc + 1:]
    c = pl.program_id(1)
    nch = pl.num_programs(1)
    width = ppc * PAGE_SIZE
    nq = DSA_KV_HEADS * SQ_ROWS

    @pl.when(c == 0)
    def _():
        m_ref[...] = jnp.full_like(m_ref, NEG_BIG)
        l_ref[...] = jnp.zeros_like(l_ref)
        acc_ref[...] = jnp.zeros_like(acc_ref)
        run_ref[...] = jnp.zeros_like(run_ref)
        r = lax.broadcasted_iota(I32, (width, width), 0)
        cc = lax.broadcasted_iota(I32, (width, width), 1)
        tri_ref[...] = (r <= cc).astype(BF16)

    tau = tau_ref[0][:, 0:1]
    need = need_ref[0][:, 0:1]
    tau_eff = jnp.maximum(tau, INT_MIN + 1)
    has_tie = jnp.max(tie_ref[0]) > 0.0
    q = q_ref[0]

    def process(keys, kb, vb):
        w = keys.shape[1]

        def fast(run_eq):
            return jnp.where(keys >= tau_eff, 0.0, NEG_BIG), run_eq

        def slow(run_eq):
            eq = (keys == tau) & (keys > INT_MIN)
            rank = run_eq + _dot(eq.astype(BF16), tri_ref[0:w, 0:w])
            sel = (keys > tau) | (eq & (rank <= need))
            return jnp.where(sel, 0.0, NEG_BIG), run_eq + jnp.sum(eq.astype(F32), axis=1, keepdims=True)

        bias, run_eq = lax.cond(has_tie, slow, fast, run_ref[...])
        run_ref[...] = run_eq
        bias_all = jnp.concatenate([bias] * (nq // SUB), axis=0)
        s = _dot_nt(q, kb) + bias_all
        m_old = m_ref[...]
        m_new = jnp.maximum(m_old, jnp.max(s, axis=1, keepdims=True))
        p = jnp.exp2(s - m_new)
        alpha = jnp.exp2(m_old - m_new)
        l_ref[...] = alpha * l_ref[...] + jnp.sum(p, axis=1, keepdims=True)
        acc_ref[...] = alpha * acc_ref[...] + _dot(p.astype(BF16), vb)
        m_ref[...] = m_new

    for j in range(ppc):
        kc_ref[j * PAGE_SIZE:(j + 1) * PAGE_SIZE, :] = kpages[j][0, 0].astype(BF16)
        vc_ref[j * PAGE_SIZE:(j + 1) * PAGE_SIZE, :] = vpages[j][0, 0].astype(BF16)
    process(keys_ref[0], kc_ref[...], vc_ref[...])

    @pl.when(c == nch - 1)
    def _():
        d = DSA_KV_HEADS * DSA_HD
        kn = jnp.concatenate([kn_ref[0], jnp.zeros((LANE - ts, d), F32)], axis=0).astype(BF16)
        vn = jnp.concatenate([vn_ref[0], jnp.zeros((LANE - ts, d), F32)], axis=0).astype(BF16)
        process(ktail_ref[0], kn, vn)
        o_ref[0] = acc_ref[...] / l_ref[...]


def dsa_sample(q_bf, qi_bf, wi, ki_new, kd_new, vd_new, cache_k, cache_v, cache_idx_k, page_table, layer, bs, ts):
    assert ts <= SUB
    n_pages = page_table.shape[1]
    past = n_pages * PAGE_SIZE
    ppc = 8 if n_pages % 8 == 0 else n_pages
    nch = n_pages // ppc
    width = ppc * PAGE_SIZE
    nk = past + LANE
    k_sel = min(TOPK_MAX, (past + ts) // 4)
    dkv = DSA_KV_HEADS * DSA_HD
    n_pool = cache_k.shape[1]
    pt = page_table.reshape(-1).astype(I32)
    qi_r = qi_bf.reshape(bs, ts * IDX_HEADS, IDX_D)
    w_r = wi[:, :IDX_HEADS].reshape(bs, ts * IDX_HEADS, 1)
    cik = cache_idx_k
    ck = cache_k.reshape(cache_k.shape[0], n_pool, PAGE_SIZE, dkv)
    cv = cache_v.reshape(cache_v.shape[0], n_pool, PAGE_SIZE, dkv)

    def page_spec(dlast, j):
        return pl.BlockSpec((1, 1, PAGE_SIZE, dlast),
                            lambda b, c, ptr, j=j: (layer, ptr[b * n_pages + c * ppc + j], 0, 0))

    per_b = lambda shape: pl.BlockSpec((1,) + shape, lambda b, c, ptr: (b,) + tuple(0 for _ in shape))
    sel_spec = pltpu.PrefetchScalarGridSpec(
        num_scalar_prefetch=1, grid=(bs, nch),
        in_specs=[per_b((ts * IDX_HEADS, IDX_D)), per_b((ts * IDX_HEADS, 1)), per_b((ts, IDX_D))]
                 + [page_spec(IDX_D, j) for j in range(ppc)],
        out_specs=[per_b((SUB, nk)), per_b((SUB, LANE)), per_b((SUB, LANE)), per_b((SUB, LANE))],
        scratch_shapes=[pltpu.VMEM((SUB, nk), I32), pltpu.VMEM((width, IDX_D), BF16)])
    keys, tau, need, tie = pl.pallas_call(
        functools.partial(_sdsa_select_kernel, ppc=ppc, ts=ts, k_sel=k_sel, past=past),
        out_shape=[jax.ShapeDtypeStruct((bs, SUB, nk), I32), jax.ShapeDtypeStruct((bs, SUB, LANE), I32),
                   jax.ShapeDtypeStruct((bs, SUB, LANE), F32), jax.ShapeDtypeStruct((bs, SUB, LANE), F32)],
        grid_spec=sel_spec,
        compiler_params=_cparams(("parallel", "arbitrary")),
        name="dsa_sample_select",
    )(pt, qi_r, w_r, ki_new.reshape(bs, ts, IDX_D), *([cik] * ppc))

    qx = q_bf.reshape(bs, ts, DSA_KV_HEADS, DSA_G, DSA_HD).transpose(0, 2, 3, 1, 4)
    qx = jnp.pad(qx, ((0, 0), (0, 0), (0, 0), (0, SUB - ts), (0, 0))).reshape(bs, DSA_KV_HEADS, SQ_ROWS, DSA_HD)
    eye = jnp.eye(DSA_KV_HEADS, dtype=BF16)
    q_bd = (qx[:, :, :, None, :] * eye[None, :, None, :, None]).reshape(bs, DSA_KV_HEADS * SQ_ROWS, dkv)
    nq = DSA_KV_HEADS * SQ_ROWS
    att_spec = pltpu.PrefetchScalarGridSpec(
        num_scalar_prefetch=1, grid=(bs, nch),
        in_specs=[per_b((nq, dkv)),
                  pl.BlockSpec((1, SUB, width), lambda b, c, ptr: (b, 0, c)),
                  pl.BlockSpec((1, SUB, LANE), lambda b, c, ptr: (b, 0, past // LANE)),
                  per_b((SUB, LANE)), per_b((SUB, LANE)), per_b((SUB, LANE)),
                  per_b((ts, dkv)), per_b((ts, dkv))]
                 + [page_spec(dkv, j) for j in range(ppc)] + [page_spec(dkv, j) for j in range(ppc)],
        out_specs=per_b((nq, dkv)),
        scratch_shapes=[pltpu.VMEM((nq, 1), F32), pltpu.VMEM((nq, 1), F32), pltpu.VMEM((nq, dkv), F32),
                        pltpu.VMEM((SUB, 1), F32), pltpu.VMEM((width, width), BF16),
                        pltpu.VMEM((width, dkv), BF16), pltpu.VMEM((width, dkv), BF16)])
    o_bd = pl.pallas_call(
        functools.partial(_sdsa_attend_kernel, ppc=ppc, ts=ts),
        out_shape=jax.ShapeDtypeStruct((bs, nq, dkv), F32),
        grid_spec=att_spec,
        compiler_params=_cparams(("parallel", "arbitrary")),
        name="dsa_sample_attend",
    )(pt, q_bd, keys, keys, tau, need, tie, kd_new.reshape(bs, ts, dkv), vd_new.reshape(bs, ts, dkv),
      *([ck] * ppc), *([cv] * ppc))
    o6 = o_bd.reshape(bs, DSA_KV_HEADS, DSA_G, SUB, DSA_KV_HEADS, DSA_HD)
    ar = jnp.arange(DSA_KV_HEADS)
    od = o6[:, ar, :, :, ar, :]
    return od[:, :, :, :ts].transpose(1, 3, 0, 2, 4).reshape(bs * ts, DSA_HEADS * DSA_HD)


def _gelu_tanh(x):
    return 0.5 * x * (1.0 + jnp.tanh(math.sqrt(2.0 / math.pi) * (x + 0.044715 * x * x * x)))


def _softplus(x):
    return jnp.maximum(x, 0.0) + jnp.log1p(jnp.exp(-jnp.abs(x)))


def _lru_gates(xc, wa, ba, wx, bx, lam):
    xcb = xc.astype(BF16)
    gate_r = _sigmoid(_dot(xcb, wa) + ba)
    gate_i = _sigmoid(_dot(xcb, wx) + bx)
    log_a = -LRU_C * gate_r * _softplus(-lam)
    th = jnp.tanh(log_a)
    mult = jnp.sqrt(-2.0 * th / (1.0 - th))
    return jnp.exp(log_a), mult, gate_i * xc


def _lru_kernel(x_ref, y_ref, cw_ref, cb_ref, wa_ref, ba_ref, wx_ref, bx_ref, lam_ref,
                o_ref, hl_ref, xp_ref, h_ref, *, tb):
    i = pl.program_id(0)

    @pl.when(i == 0)
    def _():
        xp_ref[0:8, :] = jnp.zeros((8, LRU_W), F32)
        h_ref[...] = jnp.zeros_like(h_ref)

    xp_ref[8:8 + tb, :] = x_ref[...]
    rows = lax.broadcasted_iota(I32, (tb, LRU_BW), 0)
    first = (rows == 0) & (i == 0)
    for n in range(LRU_BLOCKS):
        sl = slice(n * LRU_BW, (n + 1) * LRU_BW)
        xc = jnp.broadcast_to(cb_ref[:, sl], (tb, LRU_BW))
        for w in range(CONV_W):
            xc = xc + xp_ref[8 - (CONV_W - 1) + w:8 - (CONV_W - 1) + w + tb, sl] * cw_ref[w:w + 1, sl]
        a, mult, gx = _lru_gates(xc, wa_ref[n], ba_ref[:, sl], wx_ref[n], bx_ref[:, sl], lam_ref[:, sl])
        b = jnp.where(first, 1.0, mult) * gx
        d = 1
        while d < tb:
            a_sh = pltpu.roll(a, d, 0)
            b_sh = pltpu.roll(b, d, 0)
            ok = rows >= d
            b = jnp.where(ok, a * b_sh + b, b)
            a = jnp.where(ok, a * a_sh, a)
            d *= 2
        h = a * h_ref[:, sl] + b
        h_ref[:, sl] = h[tb - 1:tb, :]
        o_ref[:, sl] = h * _gelu_tanh(y_ref[:, sl])
    xp_ref[0:8, :] = xp_ref[tb:tb + 8, :]
    hl_ref[...] = h_ref[...]


def rglru_prompt(z, conv_w, conv_b, wa, ba, wx, bx, lam):
    t = z.shape[0]
    tb = _tile(t, 256)
    vec = lambda a: a.reshape(1, LRU_W)
    return pl.pallas_call(
        functools.partial(_lru_kernel, tb=tb),
        out_shape=[jax.ShapeDtypeStruct((t, LRU_W), F32), jax.ShapeDtypeStruct((1, LRU_W), F32)],
        grid=(t // tb,),
        in_specs=[pl.BlockSpec((tb, LRU_W), lambda i: (i, Z_LX // LRU_W)),
                  pl.BlockSpec((tb, LRU_W), lambda i: (i, Z_LY // LRU_W)),
                  _cst((CONV_W, LRU_W)), _cst((1, LRU_W)),
                  _cst((LRU_BLOCKS, LRU_BW, LRU_BW)), _cst((1, LRU_W)),
                  _cst((LRU_BLOCKS, LRU_BW, LRU_BW)), _cst((1, LRU_W)), _cst((1, LRU_W))],
        out_specs=[pl.BlockSpec((tb, LRU_W), lambda i: (i, 0)), _cst((1, LRU_W))],
        scratch_shapes=[pltpu.VMEM((tb + 8, LRU_W), F32), pltpu.VMEM((1, LRU_W), F32)],
        compiler_params=_cparams(("arbitrary",)),
        name="rglru_prompt",
    )(z, z, conv_w, vec(conv_b), wa.astype(BF16), vec(ba), wx.astype(BF16), vec(bx), vec(lam))


def _lru_sample_kernel(xs_ref, y_ref, h0_ref, cw_ref, cb_ref, wa_ref, ba_ref, wx_ref, bx_ref, lam_ref,
                       o_ref, hl_ref, *, ts, first_is_start):
    h = h0_ref[0]
    for t in range(ts):
        xc = cb_ref[...]
        for w in range(CONV_W):
            xc = xc + xs_ref[t + w] * cw_ref[w:w + 1, :]
        a, mult, gx = _lru_gates(xc, wa_ref[0], ba_ref[...], wx_ref[0], bx_ref[...], lam_ref[...])
        if first_is_start and t == 0:
            mult = jnp.ones_like(mult)
        h = a * h + mult * gx
        o_ref[t] = h * _gelu_tanh(y_ref[t])
    hl_ref[...] = h


def rglru_sample(xs, ly, state_lru, layer, conv_w, conv_b, wa, ba, wx, bx, lam, first_is_start):
    ts, b, w = ly.shape
    vec = lambda a: a.reshape(1, w)
    blk = lambda r: pl.BlockSpec((r, b, LRU_BW), lambda n: (0, 0, n))
    col = lambda r: pl.BlockSpec((r, LRU_BW), lambda n: (0, n))
    wsp = pl.BlockSpec((1, LRU_BW, LRU_BW), lambda n: (n, 0, 0))
    return pl.pallas_call(
        functools.partial(_lru_sample_kernel, ts=ts, first_is_start=first_is_start),
        out_shape=[jax.ShapeDtypeStruct((ts, b, w), F32), jax.ShapeDtypeStruct((b, w), F32)],
        grid=(LRU_BLOCKS,),
        in_specs=[blk(CONV_W - 1 + ts), blk(ts),
                  pl.BlockSpec((1, b, LRU_BW), lambda n: (layer, 0, n)),
                  col(CONV_W), col(1), wsp, col(1), wsp, col(1), col(1)],
        out_specs=[blk(ts), pl.BlockSpec((b, LRU_BW), lambda n: (0, n))],
        compiler_params=_cparams(("parallel",)),
        name="rglru_sample",
    )(xs, ly, state_lru, conv_w, vec(conv_b), wa.astype(BF16), vec(ba), wx.astype(BF16), vec(bx), vec(lam))


def _gla_gate_kernel(glr_ref, w2_ref, bg_ref, o_ref):
    x = _dot(glr_ref[...].astype(BF16), w2_ref[...]) + bg_ref[...]
    o_ref[...] = _log_sigmoid(x) * (1.0 / GLA_TAU)


def gla_gate(z, w2, bg):
    t = z.shape[0]
    w = GLA_HEADS * GLA_DK
    w2p = jnp.pad(w2, ((0, LANE - GLA_RANK), (0, 0))).astype(BF16)
    return pl.pallas_call(
        _gla_gate_kernel,
        out_shape=jax.ShapeDtypeStruct((t, w), F32),
        grid=(1,),
        in_specs=[pl.BlockSpec((t, LANE), lambda i: (0, Z_GLR // LANE)), _cst((LANE, w)), _cst((1, w))],
        out_specs=_cst((t, w)),
        compiler_params=_cparams(("arbitrary",)),
        name="gla_gate",
    )(z, w2p, bg.reshape(1, w))


def _gla_sample_kernel(qt_ref, kt_ref, gt_ref, v_ref, r_ref, s0_ref, ng_ref, o_ref, s_ref, *, ts):
    ng = ng_ref[...]
    for h in range(GLA_HEADS):
        s = s0_ref[0, 0, h]
        qt = qt_ref[0, h] * (GLA_DK ** -0.5)
        kt = kt_ref[0, h]
        dec = jnp.exp(gt_ref[0, h])
        sl = slice(h * GLA_DV, (h + 1) * GLA_DV)
        for t in range(ts):
            s = dec[:, t:t + 1] * s + kt[:, t:t + 1] * v_ref[0, t:t + 1, sl]
            o = jnp.sum(qt[:, t:t + 1] * s, axis=0, keepdims=True)
            on = o * lax.rsqrt(jnp.mean(o * o, -1, keepdims=True) + LN_EPS) * ng
            o_ref[0, t:t + 1, sl] = on * _silu(r_ref[0, t:t + 1, sl])
        s_ref[0, h] = s


def gla_sample(qt, kt, gt, v, r, state_gla, layer, ng):
    b, h, dk, ts = qt.shape
    col = pl.BlockSpec((1, h, dk, ts), lambda i: (i, 0, 0, 0))
    row = pl.BlockSpec((1, ts, h * GLA_DV), lambda i: (i, 0, 0))
    return pl.pallas_call(
        functools.partial(_gla_sample_kernel, ts=ts),
        out_shape=[jax.ShapeDtypeStruct((b, ts, h * GLA_DV), F32),
                   jax.ShapeDtypeStruct((b, h, dk, GLA_DV), F32)],
        grid=(b,),
        in_specs=[col, col, col, row, row,
                  pl.BlockSpec((1, 1, h, dk, GLA_DV), lambda i: (layer, i, 0, 0, 0)), _cst((1, GLA_DV))],
        out_specs=[row, pl.BlockSpec((1, h, dk, GLA_DV), lambda i: (i, 0, 0, 0))],
        compiler_params=_cparams(("parallel",)),
        name="gla_sample",
    )(qt, kt, gt, v, r, state_gla, ng.reshape(1, GLA_DV))


def _mem_heads(q_of, k_of, v_of, store):
    scale = MEM_HD ** -0.5
    for h in range(MEM_HEADS):
        sl = slice(h * MEM_HD, (h + 1) * MEM_HD)
        s = _dot_nt(q_of(sl).astype(BF16), k_of(sl).astype(BF16)) * scale
        p = jnp.exp(s - jnp.max(s, -1, keepdims=True))
        p = p / jnp.sum(p, -1, keepdims=True)
        store(sl, _dot(p.astype(BF16), v_of(sl).astype(BF16)))


def _mem_kernel(q_ref, mk_ref, mv_ref, o_ref):
    def store(sl, val):
        o_ref[:, sl] = val
    _mem_heads(lambda sl: q_ref[:, sl], lambda sl: mk_ref[:, sl], lambda sl: mv_ref[:, sl], store)


def mem_attend_prompt(z, mkv):
    t = z.shape[0]
    tb = _tile(t, 512)
    w = MEM_HEADS * MEM_HD
    return pl.pallas_call(
        _mem_kernel,
        out_shape=jax.ShapeDtypeStruct((t, w), F32),
        grid=(t // tb,),
        in_specs=[pl.BlockSpec((tb, w), lambda i: (i, Z_MQ // w)),
                  pl.BlockSpec((N_MEM, w), lambda i: (0, 0)),
                  pl.BlockSpec((N_MEM, w), lambda i: (0, 1))],
        out_specs=pl.BlockSpec((tb, w), lambda i: (i, 0)),
        compiler_params=_cparams(("parallel",)),
        name="mem_attend",
    )(z, mkv, mkv)


def _mem_sample_kernel(q_ref, mk_ref, mv_ref, o_ref):
    def store(sl, val):
        o_ref[0, :, sl] = val
    _mem_heads(lambda sl: q_ref[0, :, sl], lambda sl: mk_ref[0, 0, :, sl], lambda sl: mv_ref[0, 0, :, sl], store)


def mem_attend_sample(q, mem_k, mem_v, layer):
    b, rows, w = q.shape
    qs = pl.BlockSpec((1, rows, w), lambda i: (i, 0, 0))
    ms = pl.BlockSpec((1, 1, N_MEM, w), lambda i: (layer, i, 0, 0))
    return pl.pallas_call(
        _mem_sample_kernel,
        out_shape=jax.ShapeDtypeStruct((b, rows, w), F32),
        grid=(b,),
        in_specs=[qs, ms, ms],
        out_specs=qs,
        compiler_params=_cparams(("parallel",)),
        name="mem_attend_sample",
    )(q, mem_k, mem_v)


def _merge_kernel(b0_ref, b1_ref, b2_ref, b3_ref, g0_ref, g1_ref, g2_ref, g3_ref, x_ref,
                  wb_ref, bg_ref, wo_ref, lg_ref, lb_ref, o_ref, *, alpha):
    brs = (b0_ref, b1_ref, b2_ref, b3_ref)
    gls = (g0_ref, g1_ref, g2_ref, g3_ref)
    merged = None
    for j in range(N_BRANCH):
        proj = _dot(brs[j][...].astype(BF16), wb_ref[j])
        term = _sigmoid(gls[j][...] + bg_ref[j:j + 1, :]) * proj
        merged = term if merged is None else merged + term
    y = _dot(merged.astype(BF16), wo_ref[...])
    o_ref[...] = _ln_rows(alpha * x_ref[...] + y, lg_ref[...], lb_ref[...])


def merge_out(branches, z, x, wb, bgate, wo, lg, lb, alpha):
    t = x.shape[0]
    tb = _tile(t, 256)
    d = D_MODEL
    row = pl.BlockSpec((tb, d), lambda i: (i, 0))
    gl = [pl.BlockSpec((tb, d), lambda i, o=Z_GL // d + j: (i, o)) for j in range(N_BRANCH)]
    return pl.pallas_call(
        functools.partial(_merge_kernel, alpha=alpha),
        out_shape=jax.ShapeDtypeStruct((t, d), F32),
        grid=(t // tb,),
        in_specs=[row, row, row, row] + gl + [row, _cst((N_BRANCH, d, d)), _cst((N_BRANCH, d)), _cst((d, d)),
                                               _cst((1, d)), _cst((1, d))],
        out_specs=row,
        compiler_params=_cparams(("parallel",)),
        name="merge_out",
    )(*branches, z, z, z, z, x, wb, bgate, wo, lg.reshape(1, d), lb.reshape(1, d))


def _first_argmax(x, n):
    m = jnp.max(x, axis=0, keepdims=True)
    ri = lax.broadcasted_iota(I32, x.shape, 0)
    idx = jnp.min(jnp.where(x == m, ri, n), axis=0, keepdims=True)
    return m, idx, ri


def _router_kernel(x_ref, rwt_ref, rb_ref, e_ref, w_ref):
    logits = lax.dot_general(rwt_ref[...], x_ref[...], (((1,), (1,)), ((), ())),
                             preferred_element_type=F32, precision=lax.Precision.HIGHEST)
    s = _sigmoid(logits)
    sb = s + rb_ref[...]
    gsz = N_EXPERTS // N_GROUPS
    neg = -jnp.inf
    gs = []
    for g in range(N_GROUPS):
        blk = sb[g * gsz:(g + 1) * gsz, :]
        m1, i1, ri = _first_argmax(blk, gsz)
        m2 = jnp.max(jnp.where(ri == i1, neg, blk), axis=0, keepdims=True)
        gs.append(m1 + m2)
    gscore = jnp.concatenate(gs, axis=0)
    gmask = jnp.zeros(gscore.shape, jnp.bool_)
    for _ in range(TOPK_GROUPS):
        _, gi, ri = _first_argmax(gscore, N_GROUPS)
        hit = ri == gi
        gmask = gmask | hit
        gscore = jnp.where(hit, neg, gscore)
    cand = jnp.concatenate(
        [jnp.where(gmask[g:g + 1, :], sb[g * gsz:(g + 1) * gsz, :], neg) for g in range(N_GROUPS)], axis=0)
    idxs, ws = [], []
    for _ in range(TOP_K):
        _, ei, ri = _first_argmax(cand, N_EXPERTS)
        hit = ri == ei
        idxs.append(ei)
        ws.append(jnp.sum(jnp.where(hit, s, 0.0), axis=0, keepdims=True))
        cand = jnp.where(hit, neg, cand)
    w = jnp.concatenate(ws, axis=0)
    e_ref[...] = jnp.concatenate(idxs, axis=0)
    w_ref[...] = w / jnp.sum(w, axis=0, keepdims=True) * ROUTED_SCALE


def moe_router(x, router_w, router_b):
    t, d = x.shape
    tb = _tile(t, 512)
    if tb % LANE != 0:
        tb = t
    return pl.pallas_call(
        _router_kernel,
        out_shape=[jax.ShapeDtypeStruct((TOP_K, t), I32), jax.ShapeDtypeStruct((TOP_K, t), F32)],
        grid=(t // tb,),
        in_specs=[pl.BlockSpec((tb, d), lambda i: (i, 0)), _cst((N_EXPERTS, d)), _cst((N_EXPERTS, 1))],
        out_specs=[pl.BlockSpec((TOP_K, tb), lambda i: (0, i)), pl.BlockSpec((TOP_K, tb), lambda i: (0, i))],
        compiler_params=_cparams(("parallel",)),
        name="moe_router",
    )(x, router_w.T, router_b.reshape(N_EXPERTS, 1))


def _experts_kernel(be_ref, nv_ref, x_ref, rw_ref, w1_ref, w3_ref, w2_ref, o_ref, w1b, w3b, w2b):
    i = pl.program_id(0)
    e = be_ref[i]
    prev = be_ref[jnp.maximum(i - 1, 0)]

    @pl.when((i == 0) | (e != prev))
    def _():
        w1b[...] = w1_ref[0].astype(BF16)
        w3b[...] = w3_ref[0].astype(BF16)
        w2b[...] = w2_ref[0].astype(BF16)

    @pl.when(i < nv_ref[0])
    def _():
        x = x_ref[...]
        h = _silu(_dot(x, w1b[...])) * _dot(x, w3b[...])
        o_ref[...] = _dot(h.astype(BF16), w2b[...]) * rw_ref[...]

    @pl.when(i >= nv_ref[0])
    def _():
        o_ref[...] = jnp.zeros_like(o_ref)


def moe_experts(xp, row_w, block_e, n_valid, w1, w3, w2):
    p, d = xp.shape
    de = w1.shape[2]
    nb = p // MOE_BLK
    grid_spec = pltpu.PrefetchScalarGridSpec(
        num_scalar_prefetch=2,
        grid=(nb,),
        in_specs=[pl.BlockSpec((MOE_BLK, d), lambda i, be, nv: (i, 0)),
                  pl.BlockSpec((MOE_BLK, 1), lambda i, be, nv: (i, 0)),
                  pl.BlockSpec((1, d, de), lambda i, be, nv: (be[i], 0, 0)),
                  pl.BlockSpec((1, d, de), lambda i, be, nv: (be[i], 0, 0)),
                  pl.BlockSpec((1, de, d), lambda i, be, nv: (be[i], 0, 0))],
        out_specs=pl.BlockSpec((MOE_BLK, d), lambda i, be, nv: (i, 0)),
        scratch_shapes=[pltpu.VMEM((d, de), BF16), pltpu.VMEM((d, de), BF16), pltpu.VMEM((de, d), BF16)])
    return pl.pallas_call(
        _experts_kernel,
        out_shape=jax.ShapeDtypeStruct((p, d), F32),
        grid_spec=grid_spec,
        compiler_params=_cparams(("arbitrary",)),
        name="moe_experts",
    )(block_e, n_valid, xp, row_w, w1, w3, w2)


def _moe_final_kernel(x_ref, r_ref, w1_ref, w3_ref, w2_ref, lg_ref, lb_ref, o_ref, *, alpha):
    x = x_ref[...]
    xb = x.astype(BF16)
    h = _silu(_dot(xb, w1_ref[...])) * _dot(xb, w3_ref[...])
    shared = _dot(h.astype(BF16), w2_ref[...])
    o_ref[...] = _ln_rows(alpha * x + (r_ref[...] + shared), lg_ref[...], lb_ref[...])


def moe_final(x, routed, w1, w3, w2, lg, lb, alpha):
    t, d = x.shape
    tb = _tile(t, 512)
    ds = w1.shape[1]
    row = pl.BlockSpec((tb, d), lambda i: (i, 0))
    return pl.pallas_call(
        functools.partial(_moe_final_kernel, alpha=alpha),
        out_shape=jax.ShapeDtypeStruct((t, d), F32),
        grid=(t // tb,),
        in_specs=[row, row, _cst((d, ds)), _cst((d, ds)), _cst((ds, d)), _cst((1, d)), _cst((1, d))],
        out_specs=row,
        compiler_params=_cparams(("parallel",)),
        name="moe_final",
    )(x, routed, w1.astype(BF16), w3.astype(BF16), w2.astype(BF16), lg.reshape(1, d), lb.reshape(1, d))


def moe_layer(x, lw, alpha):
    n, d = x.shape
    eidx_t, wsel_t = moe_router(x, lw['router_w'], lw['router_bias'])
    m = n * TOP_K
    flat_e = eidx_t.T.reshape(m)
    flat_w = wsel_t.T.reshape(m)
    nb = (m + N_EXPERTS * (MOE_BLK - 1) + MOE_BLK - 1) // MOE_BLK
    p = nb * MOE_BLK
    order = jnp.argsort(flat_e)
    se = flat_e[order]
    counts = jnp.bincount(flat_e, length=N_EXPERTS)
    pcounts = (counts + MOE_BLK - 1) // MOE_BLK * MOE_BLK
    starts = jnp.cumsum(counts) - counts
    pends = jnp.cumsum(pcounts)
    pstarts = pends - pcounts
    dest_sorted = pstarts[se] + (jnp.arange(m) - starts[se])
    row_tok = jnp.zeros((p,), I32).at[dest_sorted].set((order // TOP_K).astype(I32))
    row_w = jnp.zeros((p,), F32).at[dest_sorted].set(flat_w[order])
    dest = jnp.zeros((m,), I32).at[order].set(dest_sorted.astype(I32))
    block_e = jnp.minimum(jnp.searchsorted(pends, jnp.arange(nb) * MOE_BLK, side='right'), N_EXPERTS - 1).astype(I32)
    n_valid = (pends[-1] // MOE_BLK).astype(I32).reshape(1)
    xp = x.astype(BF16)[row_tok]
    yp = moe_experts(xp, row_w.reshape(p, 1), block_e, n_valid, lw['exp_w1'], lw['exp_w3'], lw['exp_w2'])
    routed = jnp.sum(yp[dest.reshape(n, TOP_K)], axis=1)
    return moe_final(x, routed, lw['sh_w1'], lw['sh_w3'], lw['sh_w2'], lw['ln2_g'], lw['ln2_b'], alpha)


def _pack_w_in(w):
    parts, start = [], 0
    for s in IN_SIZES:
        parts.append(w[:, start:start + s])
        start += s
    gq, gk, gv, gr, glr, dq, dk, dv, iq, ik, iw, lx, ly, mq, gl = parts
    d = w.shape[0]
    padc = lambda a, n: jnp.pad(a, ((0, 0), (0, n - a.shape[1])))
    cols = [gq, gk, gv, gr, dq, dk, dv, iq, padc(glr, LANE), padc(jnp.concatenate([ik, iw], 1), LANE),
            jnp.zeros((d, Z_LX - Z_MISC - LANE), w.dtype), lx, ly, mq, gl]
    out = jnp.concatenate(cols, axis=1).astype(BF16)
    assert out.shape[1] == Z_W
    return out


def _sample_mixers(zs, hs_tabs, l, bs, ts, past, cache_k, cache_v, cache_idx_k, cache_mem_k, cache_mem_v,
                   state_gla, state_conv, state_lru, page_table, p):
    n = bs * ts
    q_bf, kd, _, vd, _, qi_bf, ki, _, wi = dsa_prep(zs, hs_tabs, p['idx_ln_g'], p['idx_ln_b'])
    o_dsa = dsa_sample(q_bf, qi_bf, wi, ki, kd, vd, cache_k, cache_v, cache_idx_k, page_table, l, bs, ts)
    glog = gla_gate(zs, p['gla_w_gate2'], p['gla_b_gate'])
    colz = lambda a: a.reshape(bs, ts, GLA_HEADS, GLA_DK).transpose(0, 2, 3, 1)
    qt = colz(zs[:, Z_GQ:Z_GQ + GLA_HEADS * GLA_DK])
    kt = colz(zs[:, Z_GK:Z_GK + GLA_HEADS * GLA_DK])
    gt = colz(glog)
    rowz = lambda off: zs[:, off:off + GLA_HEADS * GLA_DV].reshape(bs, ts, GLA_HEADS * GLA_DV)
    o_gla, s_new = gla_sample(qt, kt, gt, rowz(Z_GV), rowz(Z_GR), state_gla, l, p['gla_norm_g'])
    lx = zs[:, Z_LX:Z_LX + LRU_W].reshape(bs, ts, LRU_W)
    ly = zs[:, Z_LY:Z_LY + LRU_W].reshape(bs, ts, LRU_W)
    xp = jnp.concatenate([state_conv[l], lx], axis=1)
    o_lru, h_new = rglru_sample(xp.transpose(1, 0, 2), ly.transpose(1, 0, 2), state_lru, l,
                                p['conv_w'], p['conv_b'], p['lru_wa'], p['lru_ba'], p['lru_wx'], p['lru_bx'],
                                p['lru_lambda'], first_is_start=(past == 0))
    conv_new = xp[:, xp.shape[1] - (CONV_W - 1):]
    w = MEM_HEADS * MEM_HD
    rows = -(-ts // SQ_ROWS) * SQ_ROWS
    mq = jnp.pad(zs[:, Z_MQ:Z_MQ + w].reshape(bs, ts, w), ((0, 0), (0, rows - ts), (0, 0)))
    nl = cache_mem_k.shape[0]
    o_mem = mem_attend_sample(mq, cache_mem_k.reshape(nl, bs, N_MEM, w), cache_mem_v.reshape(nl, bs, N_MEM, w), l)
    o_mem = o_mem[:, :ts]
    branches = (o_gla.reshape(n, -1), o_dsa, o_lru.transpose(1, 0, 2).reshape(n, -1), o_mem.reshape(n, -1))
    states = (kd.reshape(bs, ts, DSA_KV_HEADS, DSA_HD), vd.reshape(bs, ts, DSA_KV_HEADS, DSA_HD),
              ki.reshape(bs, ts, IDX_D), s_new, conv_new, h_new)
    return branches, states


def kernel(x_prompt, x_sample, cache_k, cache_v, cache_idx_k, cache_mem_k, cache_mem_v, state_gla, state_conv, state_lru, page_table, mem_prompt, ln_in_g, ln_in_b, w_in, b_gate, gla_w_gate2, gla_b_gate, gla_norm_g, idx_ln_g, idx_ln_b, conv_w, conv_b, lru_wa, lru_ba, lru_wx, lru_bx, lru_lambda, mem_w_kv, w_branch, w_out, ln1_g, ln1_b, ln2_g, ln2_b, router_w, router_bias, exp_w1, exp_w3, exp_w2, sh_w1, sh_w3, sh_w2):
    bp, sp_len, d = x_prompt.shape
    assert bp == 1, "prompt group is a single sequence"
    bs, ts, _ = x_sample.shape
    depth = w_in.shape[0]
    alpha = (2 * depth) ** 0.25
    past = page_table.shape[1] * PAGE_SIZE
    tabs_p = rope_tables(jnp.arange(sp_len, dtype=I32))
    tabs_s = rope_tables(jnp.tile(past + jnp.arange(ts, dtype=I32), bs))

    hp = layer_norm_rows(x_prompt.reshape(sp_len, d), ln_in_g, ln_in_b)
    hs = layer_norm_rows(x_sample.reshape(bs * ts, d), ln_in_g, ln_in_b)

    st_p, st_s, mem_new = [], [], []
    for l in range(depth):
        lw = {'ln2_g': ln2_g[l], 'ln2_b': ln2_b[l], 'router_w': router_w[l], 'router_bias': router_bias[l],
              'exp_w1': exp_w1[l], 'exp_w3': exp_w3[l], 'exp_w2': exp_w2[l],
              'sh_w1': sh_w1[l], 'sh_w3': sh_w3[l], 'sh_w2': sh_w2[l]}
        mp = {'gla_w_gate2': gla_w_gate2[l], 'gla_b_gate': gla_b_gate[l], 'gla_norm_g': gla_norm_g[l],
              'idx_ln_g': idx_ln_g[l], 'idx_ln_b': idx_ln_b[l],
              'conv_w': conv_w[l], 'conv_b': conv_b[l], 'lru_wa': lru_wa[l], 'lru_ba': lru_ba[l],
              'lru_wx': lru_wx[l], 'lru_bx': lru_bx[l], 'lru_lambda': lru_lambda[l]}
        w_in_p = _pack_w_in(w_in[l])
        wb = w_branch[l].astype(BF16)
        wo = w_out[l].astype(BF16)

        zp = matmul(hp, w_in_p)
        mkv = matmul(mem_prompt.reshape(N_MEM, d), mem_w_kv[l].astype(BF16))
        q_bf, kd, k_bf, vd, v_bf, qi_bf, ki, ki_bf, wi = dsa_prep(zp, tabs_p, idx_ln_g[l], idx_ln_b[l])
        o_gla, s_gla = gla_prompt(zp, gla_w_gate2[l], gla_b_gate[l], gla_norm_g[l])
        o_dsa = dsa_prompt(q_bf, qi_bf, wi, ki_bf.T, k_bf.T, v_bf)
        o_lru, h_last = rglru_prompt(zp, conv_w[l], conv_b[l], lru_wa[l], lru_ba[l], lru_wx[l], lru_bx[l], lru_lambda[l])
        o_mem = mem_attend_prompt(zp, mkv)
        xp1 = merge_out((o_gla, o_dsa, o_lru, o_mem), zp, hp, wb, b_gate[l], wo, ln1_g[l], ln1_b[l], alpha)
        lx_p = zp[:, Z_LX:Z_LX + LRU_W]
        conv_p = jnp.concatenate([jnp.zeros((CONV_W - 1, LRU_W), F32), lx_p], 0)[-(CONV_W - 1):]
        st_p.append((kd.reshape(1, sp_len, DSA_KV_HEADS, DSA_HD), vd.reshape(1, sp_len, DSA_KV_HEADS, DSA_HD),
                     ki.reshape(1, sp_len, IDX_D), s_gla[None], conv_p[None], h_last.reshape(1, LRU_W)))
        mem_new.append((mkv[:, :MEM_HEADS * MEM_HD].reshape(1, N_MEM, MEM_HEADS, MEM_HD),
                        mkv[:, MEM_HEADS * MEM_HD:].reshape(1, N_MEM, MEM_HEADS, MEM_HD)))

        zs = matmul(hs, w_in_p)
        brs, sts = _sample_mixers(zs, tabs_s, l, bs, ts, past, cache_k, cache_v, cache_idx_k, cache_mem_k,
                                  cache_mem_v, state_gla, state_conv, state_lru, page_table, mp)
        xs1 = merge_out(brs, zs, hs, wb, b_gate[l], wo, ln1_g[l], ln1_b[l], alpha)
        st_s.append(sts)

        x_all = moe_layer(jnp.concatenate([xp1, xs1], 0), lw, alpha)
        hp, hs = x_all[:sp_len], x_all[sp_len:]

    stk = lambda seq, j: jnp.stack([t[j] for t in seq], axis=0)
    k_p, v_p, ik_p, gla_p, conv_p_, lru_p = [stk(st_p, j) for j in range(6)]
    k_s, v_s, ik_s, gla_s, conv_s, lru_s = [stk(st_s, j) for j in range(6)]
    memk_p, memv_p = stk(mem_new, 0), stk(mem_new, 1)
    return (hp.reshape(1, sp_len, d), hs.reshape(bs, ts, d), k_p, v_p, ik_p, gla_p, conv_p_, lru_p, memk_p, memv_p,
            k_s, v_s, ik_s, gla_s, conv_s, lru_s)
```

```python
import functools
import math

import jax
import jax.numpy as jnp
from jax import lax
from jax.experimental import pallas as pl
from jax.experimental.pallas import tpu as pltpu

F32 = jnp.float32
BF16 = jnp.bfloat16
I32 = jnp.int32

D_MODEL = 1024
PAGE_SIZE = 128
N_BRANCH = 4
GLA_HEADS = 4
GLA_DK = 128
GLA_DV = 256
GLA_RANK = 16
GLA_TAU = 16.0
GLA_CHUNK = 64
DSA_HEADS = 8
DSA_KV_HEADS = 4
DSA_HD = 128
DSA_G = DSA_HEADS // DSA_KV_HEADS
IDX_HEADS = 8
IDX_D = 64
TOPK_MAX = 256
LRU_W = 1024
LRU_BLOCKS = 4
LRU_BW = LRU_W // LRU_BLOCKS
CONV_W = 4
LRU_C = 8.0
N_MEM = 256
MEM_HEADS = 4
MEM_HD = 256
N_EXPERTS = 256
TOP_K = 8
N_GROUPS = 8
TOPK_GROUPS = 4
ROUTED_SCALE = 2.5
ROPE_THETA = 10000.0
LN_EPS = 1e-5

IN_SIZES = (GLA_HEADS * GLA_DK, GLA_HEADS * GLA_DK, GLA_HEADS * GLA_DV, GLA_HEADS * GLA_DV, GLA_RANK,
            DSA_HEADS * DSA_HD, DSA_KV_HEADS * DSA_HD, DSA_KV_HEADS * DSA_HD,
            IDX_HEADS * IDX_D, IDX_D, IDX_HEADS,
            LRU_W, LRU_W, MEM_HEADS * MEM_HD, N_BRANCH * D_MODEL)

LANE = 128
SUB = 8
Z_GQ, Z_GK, Z_GV, Z_GR = 0, 512, 1024, 2048
Z_DQ, Z_DK, Z_DV, Z_IQ = 3072, 4096, 4608, 5120
Z_GLR, Z_MISC = 5632, 5760
Z_LX, Z_LY, Z_MQ, Z_GL = 6144, 7168, 8192, 9216
Z_W = 13312

VMEM_LIMIT = 56 * 1024 * 1024
MOE_BLK = 256
INT_MIN = -2 ** 31
NEG_BIG = -1e30
LOG2E = 1.4426950408889634
SQ_ROWS = 2 * SUB


def _tile(n, target):
    if n <= target:
        return n
    for t in range(target, 7, -1):
        if n % t == 0 and t % 8 == 0:
            return t
    return n


def _cparams(sem, vmem=None):
    return pltpu.CompilerParams(dimension_semantics=sem, vmem_limit_bytes=vmem or VMEM_LIMIT)


def _ln_rows(x, g, b):
    mu = jnp.mean(x, -1, keepdims=True)
    xc = x - mu
    var = jnp.mean(xc * xc, -1, keepdims=True)
    return xc * lax.rsqrt(var + LN_EPS) * g + b


def _sigmoid(x):
    return 1.0 / (1.0 + jnp.exp(-x))


def _silu(x):
    return x * _sigmoid(x)


def _dot(a, b):
    return jnp.dot(a, b, preferred_element_type=F32)


def _dot_nt(a, b):
    return lax.dot_general(a, b, (((1,), (1,)), ((), ())), preferred_element_type=F32)


def _dot_tn(a, b):
    return lax.dot_general(a, b, (((0,), (0,)), ((), ())), preferred_element_type=F32)


def _cst(shape):
    return pl.BlockSpec(shape, lambda *_: tuple(0 for _ in shape))


def _ln_kernel(x_ref, g_ref, b_ref, o_ref):
    o_ref[...] = _ln_rows(x_ref[...], g_ref[...], b_ref[...])


def layer_norm_rows(x, g, b):
    m, d = x.shape
    tm = _tile(m, 512)
    return pl.pallas_call(
        _ln_kernel,
        out_shape=jax.ShapeDtypeStruct((m, d), F32),
        grid=(m // tm,),
        in_specs=[pl.BlockSpec((tm, d), lambda i: (i, 0)), _cst((1, d)), _cst((1, d))],
        out_specs=pl.BlockSpec((tm, d), lambda i: (i, 0)),
        compiler_params=_cparams(("parallel",)),
        name="ln_rows",
    )(x, g.reshape(1, d), b.reshape(1, d))


def _mm_kernel(x_ref, w_ref, o_ref, xb_ref):
    @pl.when(pl.program_id(1) == 0)
    def _():
        xb_ref[...] = x_ref[...].astype(BF16)

    o_ref[...] = _dot(xb_ref[...], w_ref[...])


def matmul(x, w):
    m, k = x.shape
    n = w.shape[1]
    tm = _tile(m, 1024)
    tn = _tile(n, 1024)
    return pl.pallas_call(
        _mm_kernel,
        out_shape=jax.ShapeDtypeStruct((m, n), F32),
        grid=(m // tm, n // tn),
        in_specs=[pl.BlockSpec((tm, k), lambda i, j: (i, 0)),
                  pl.BlockSpec((k, tn), lambda i, j: (0, j))],
        out_specs=pl.BlockSpec((tm, tn), lambda i, j: (i, j)),
        scratch_shapes=[pltpu.VMEM((tm, k), BF16)],
        compiler_params=_cparams(("parallel", "arbitrary")),
        name="matmul",
    )(x, w)


def _rot_half(x, width):
    if width == LANE:
        return pltpu.roll(x, LANE // 2, 1)
    half = width // 2
    lane = lax.broadcasted_iota(I32, x.shape, 1)
    first = (lane % width) < half
    return jnp.where(first, pltpu.roll(x, LANE - half, 1), pltpu.roll(x, half, 1))


def _prep_kernel(dq_ref, dk_ref, dv_ref, iq_ref, misc_ref, c128_ref, s128_ref, c64_ref, s64_ref,
                 ig_ref, ib_ref,
                 q_ref, k_ref, kb_ref, v_ref, vb_ref, qi_ref, ki_ref, kib_ref, wi_ref):
    c128, s128 = c128_ref[...], s128_ref[...]
    c64, s64 = c64_ref[...], s64_ref[...]
    qscale = (DSA_HD ** -0.5) * LOG2E
    for h in range(DSA_HEADS):
        sl = slice(h * LANE, (h + 1) * LANE)
        x = dq_ref[:, sl]
        q_ref[:, sl] = ((x * c128 + _rot_half(x, LANE) * s128) * qscale).astype(BF16)
    for h in range(DSA_KV_HEADS):
        sl = slice(h * LANE, (h + 1) * LANE)
        x = dk_ref[:, sl]
        r = x * c128 + _rot_half(x, LANE) * s128
        k_ref[:, sl] = r
        kb_ref[:, sl] = r.astype(BF16)
    v = dv_ref[...]
    v_ref[...] = v
    vb_ref[...] = v.astype(BF16)
    for c in range(IDX_HEADS * IDX_D // LANE):
        sl = slice(c * LANE, (c + 1) * LANE)
        x = iq_ref[:, sl]
        qi_ref[:, sl] = (x * c64 + _rot_half(x, IDX_D) * s64).astype(BF16)
    misc = misc_ref[...]
    lane = lax.broadcasted_iota(I32, misc.shape, 1)
    isk = lane < IDX_D
    mu = jnp.sum(jnp.where(isk, misc, 0.0), -1, keepdims=True) * (1.0 / IDX_D)
    xc = jnp.where(isk, misc - mu, 0.0)
    var = jnp.sum(xc * xc, -1, keepdims=True) * (1.0 / IDX_D)
    kn = xc * lax.rsqrt(var + LN_EPS) * ig_ref[...] + ib_ref[...]
    kr = kn * c64 + _rot_half(kn, IDX_D) * s64
    ki_ref[...] = kr[:, :IDX_D]
    kib_ref[...] = kr[:, :IDX_D].astype(BF16)
    w = pltpu.roll(misc, LANE - IDX_D, 1)
    wi_ref[...] = jnp.where(lane < IDX_HEADS, w * (IDX_HEADS ** -0.5) * (IDX_D ** -0.5), 0.0)


def dsa_prep(z, tabs, idx_g, idx_b):
    t = z.shape[0]
    tb = _tile(t, 512)
    c128, s128, c64, s64 = tabs
    zspec = lambda w, off: pl.BlockSpec((tb, w), lambda i, o=off // w: (i, o))
    row = lambda w: pl.BlockSpec((tb, w), lambda i: (i, 0))
    pad = lambda a: jnp.pad(a, (0, LANE - IDX_D)).reshape(1, LANE)
    return pl.pallas_call(
        _prep_kernel,
        out_shape=[jax.ShapeDtypeStruct((t, 1024), BF16),
                   jax.ShapeDtypeStruct((t, 512), F32),
                   jax.ShapeDtypeStruct((t, 512), BF16),
                   jax.ShapeDtypeStruct((t, 512), F32),
                   jax.ShapeDtypeStruct((t, 512), BF16),
                   jax.ShapeDtypeStruct((t, 512), BF16),
                   jax.ShapeDtypeStruct((t, IDX_D), F32),
                   jax.ShapeDtypeStruct((t, IDX_D), BF16),
                   jax.ShapeDtypeStruct((t, LANE), F32)],
        grid=(t // tb,),
        in_specs=[zspec(1024, Z_DQ), zspec(512, Z_DK), zspec(512, Z_DV), zspec(512, Z_IQ), zspec(LANE, Z_MISC),
                  row(LANE), row(LANE), row(LANE), row(LANE), _cst((1, LANE)), _cst((1, LANE))],
        out_specs=[row(1024), row(512), row(512), row(512), row(512), row(512), row(IDX_D), row(IDX_D), row(LANE)],
        compiler_params=_cparams(("parallel",)),
        name="dsa_prep",
    )(z, z, z, z, z, c128, s128, c64, s64, pad(idx_g), pad(idx_b))


def rope_tables(pos):
    posf = pos.astype(F32)[:, None]

    def tab(width):
        half = width // 2
        inv = ROPE_THETA ** (-jnp.arange(half, dtype=F32) / half)
        ang = posf * inv
        c, s = jnp.cos(ang), jnp.sin(ang)
        reps = LANE // width
        return jnp.tile(jnp.concatenate([c, c], 1), (1, reps)), jnp.tile(jnp.concatenate([-s, s], 1), (1, reps))

    c128, s128 = tab(DSA_HD)
    c64, s64 = tab(IDX_D)
    return c128, s128, c64, s64


def _log_sigmoid(x):
    return jnp.minimum(x, 0.0) - jnp.log1p(jnp.exp(-jnp.abs(x)))


def _gla_kernel(q_ref, k_ref, v_ref, r_ref, glr_ref, w2_ref, bg_ref, ng_ref, o_ref, st_ref, s_ref, *, tb):
    c = GLA_CHUNK
    j = pl.program_id(1)

    @pl.when(j == 0)
    def _():
        s_ref[...] = jnp.zeros_like(s_ref)

    ri = lax.broadcasted_iota(I32, (c, c), 0)
    ci = lax.broadcasted_iota(I32, (c, c), 1)
    causal = ci <= ri
    tri = causal.astype(F32)
    w2 = w2_ref[...]
    bg = bg_ref[...]
    ng = ng_ref[...]

    def chunk(ic, carry):
        r0 = pl.multiple_of(ic * c, c)
        q = q_ref[pl.ds(r0, c), :] * (GLA_DK ** -0.5)
        k = k_ref[pl.ds(r0, c), :]
        v = v_ref[pl.ds(r0, c), :].astype(BF16)
        x = _dot(glr_ref[pl.ds(r0, c), :].astype(BF16), w2) + bg
        g = _log_sigmoid(x) * (1.0 / GLA_TAU)
        b = jnp.dot(tri, g, preferred_element_type=F32, precision=lax.Precision.HIGHEST)
        bm = b[c // 2 - 1:c // 2, :]
        bl = b[c - 1:c, :]
        st = s_ref[...]
        qe = (q * jnp.exp(b - bm)).astype(BF16)
        ke = (k * jnp.exp(bm - b)).astype(BF16)
        a = jnp.where(causal, _dot_nt(qe, ke), 0.0)
        o = _dot_nt((q * jnp.exp(b)).astype(BF16), st.astype(BF16)) + _dot(a.astype(BF16), v)
        kd = (k * jnp.exp(bl - b)).astype(BF16)
        s_ref[...] = st * jnp.exp(bl) + _dot_tn(v, kd)
        on = o * lax.rsqrt(jnp.mean(o * o, -1, keepdims=True) + LN_EPS) * ng
        o_ref[pl.ds(r0, c), :] = on * _silu(r_ref[pl.ds(r0, c), :])
        return carry

    lax.fori_loop(0, tb // c, chunk, 0)

    @pl.when(j == pl.num_programs(1) - 1)
    def _():
        st_ref[0] = s_ref[...]


def gla_prompt(z, w2, bg, ng):
    t = z.shape[0]
    tb = _tile(t, 512)
    assert tb % GLA_CHUNK == 0
    h = GLA_HEADS
    w2p = jnp.pad(w2, ((0, LANE - GLA_RANK), (0, 0))).astype(BF16)
    o, st = pl.pallas_call(
        functools.partial(_gla_kernel, tb=tb),
        out_shape=[jax.ShapeDtypeStruct((t, h * GLA_DV), F32),
                   jax.ShapeDtypeStruct((h, GLA_DV, GLA_DK), F32)],
        grid=(h, t // tb),
        in_specs=[pl.BlockSpec((tb, GLA_DK), lambda hh, j: (j, Z_GQ // GLA_DK + hh)),
                  pl.BlockSpec((tb, GLA_DK), lambda hh, j: (j, Z_GK // GLA_DK + hh)),
                  pl.BlockSpec((tb, GLA_DV), lambda hh, j: (j, Z_GV // GLA_DV + hh)),
                  pl.BlockSpec((tb, GLA_DV), lambda hh, j: (j, Z_GR // GLA_DV + hh)),
                  pl.BlockSpec((tb, LANE), lambda hh, j: (j, Z_GLR // LANE)),
                  pl.BlockSpec((LANE, GLA_DK), lambda hh, j: (0, hh)),
                  pl.BlockSpec((1, GLA_DK), lambda hh, j: (0, hh)),
                  pl.BlockSpec((1, GLA_DV), lambda hh, j: (0, 0))],
        out_specs=[pl.BlockSpec((tb, GLA_DV), lambda hh, j: (j, hh)),
                   pl.BlockSpec((1, GLA_DV, GLA_DK), lambda hh, j: (hh, 0, 0))],
        scratch_shapes=[pltpu.VMEM((GLA_DV, GLA_DK), F32)],
        compiler_params=_cparams(("parallel", "arbitrary")),
        name="gla_prompt",
    )(z, z, z, z, z, w2p, bg.reshape(1, -1), ng.reshape(1, -1))
    return o, jnp.swapaxes(st, 1, 2)


def _sort_key(x):
    bits = lax.bitcast_convert_type(x + 0.0, I32)
    return bits ^ ((bits >> 31) & 0x7FFFFFFF)


def _kth_largest_key(count_ge, k_sel):
    kf = float(k_sel)
    lo = jnp.where(count_ge(0) >= kf, 0, INT_MIN).astype(I32)

    def bit_step(it, lo):
        cand = lo + jnp.left_shift(jnp.int32(1), 30 - it)
        return jnp.where(count_ge(cand) >= kf, cand, lo)

    tau = lax.fori_loop(0, 31, bit_step, lo)
    n_ge = count_ge(tau)
    n_gt = count_ge(tau + 1)
    need = kf - n_gt
    tie = jnp.where(tau > INT_MIN, n_ge - n_gt - need, 0.0)
    return tau, need, tie


def _dsa_kernel(q_ref, qi_ref, wi_ref, kit_ref, kt_ref, v_ref, o_ref,
                key_ref, tri_ref, q2_ref, wb_ref, m_ref, l_ref, acc_ref, *, tq, tk, ts, k_sel):
    i = pl.program_id(0)

    @pl.when(i == 0)
    def _():
        r = lax.broadcasted_iota(I32, (tk, tk), 0)
        c = lax.broadcasted_iota(I32, (tk, tk), 1)
        tri_ref[...] = (r <= c).astype(BF16)

    t0 = i * tq
    n_kt = (t0 + tq + tk - 1) // tk
    wi = wi_ref[...]
    for n in range(DSA_KV_HEADS):
        for g in range(DSA_G):
            h = n * DSA_G + g
            q2_ref[n, g * tq:(g + 1) * tq, :] = q_ref[:, h * DSA_HD:(h + 1) * DSA_HD]

    row_s = t0 + lax.broadcasted_iota(I32, (tq, ts), 0)

    for h in range(IDX_HEADS):
        wb_ref[h] = jnp.broadcast_to(wi[:, h:h + 1], (tq, LANE))

    def score_chunk(kc, carry):
        c0 = pl.multiple_of(kc * ts, ts)
        acc = jnp.zeros((tq, ts), F32)
        for h in range(IDX_HEADS):
            sc = _dot(qi_ref[:, h * IDX_D:(h + 1) * IDX_D], kit_ref[:, pl.ds(c0, ts)])
            acc = acc + jnp.maximum(sc, 0.0) * jnp.concatenate([wb_ref[h]] * (ts // LANE), axis=1)
        col = c0 + lax.broadcasted_iota(I32, (tq, ts), 1)
        key_ref[:, pl.ds(c0, ts)] = jnp.where(col <= row_s, _sort_key(acc), INT_MIN)
        return carry

    lax.fori_loop(0, n_kt * (tk // ts), score_chunk, 0)

    nl = tk // LANE

    def count_ge(cand):
        cb = jnp.broadcast_to(cand, (tq, LANE))

        def body(kt, acc):
            c0 = pl.multiple_of(kt * tk, tk)
            for j in range(nl):
                acc = acc + jnp.where(key_ref[:, pl.ds(c0 + j * LANE, LANE)] >= cb, 1, 0)
            return acc

        acc = lax.fori_loop(0, n_kt, body, jnp.zeros((tq, LANE), I32))
        return jnp.sum(acc.astype(F32), axis=1, keepdims=True)

    tau, need, tie = _kth_largest_key(count_ge, k_sel)
    tau_eff = jnp.maximum(tau, INT_MIN + 1)
    has_tie = jnp.max(tie) > 0.0

    m_ref[...] = jnp.full_like(m_ref, NEG_BIG)
    l_ref[...] = jnp.zeros_like(l_ref)
    acc_ref[...] = jnp.zeros_like(acc_ref)

    def attend(kt, run_eq):
        c0 = pl.multiple_of(kt * tk, tk)
        keys = key_ref[:, pl.ds(c0, tk)]

        def fast(run_eq):
            return jnp.where(keys >= tau_eff, 0.0, NEG_BIG), run_eq

        def slow(run_eq):
            eq = (keys == tau) & (keys > INT_MIN)
            rank = run_eq + _dot(eq.astype(BF16), tri_ref[...])
            sel = (keys > tau) | (eq & (rank <= need))
            return jnp.where(sel, 0.0, NEG_BIG), run_eq + jnp.sum(eq.astype(F32), axis=1, keepdims=True)

        bias, run_eq = lax.cond(has_tie, slow, fast, run_eq)
        bias2 = jnp.concatenate([bias] * DSA_G, axis=0)
        for n in range(DSA_KV_HEADS):
            ktile = kt_ref[n * DSA_HD:(n + 1) * DSA_HD, pl.ds(c0, tk)]
            vtile = v_ref[pl.ds(c0, tk), n * DSA_HD:(n + 1) * DSA_HD]
            s = _dot(q2_ref[n], ktile) + bias2
            m_old = m_ref[n]
            m_new = jnp.maximum(m_old, jnp.max(s, axis=1, keepdims=True))
            p = jnp.exp2(s - m_new)
            alpha = jnp.exp2(m_old - m_new)
            l_ref[n] = alpha * l_ref[n] + jnp.sum(p, axis=1, keepdims=True)
            acc_ref[n] = alpha * acc_ref[n] + _dot(p.astype(BF16), vtile)
            m_ref[n] = m_new
        return run_eq

    lax.fori_loop(0, n_kt, attend, jnp.zeros((tq, 1), F32))
    for n in range(DSA_KV_HEADS):
        o = acc_ref[n] / l_ref[n]
        for g in range(DSA_G):
            h = n * DSA_G + g
            o_ref[:, h * DSA_HD:(h + 1) * DSA_HD] = o[g * tq:(g + 1) * tq, :]


def dsa_prompt(q_bf, qi_bf, wi, kit_bf, kt_bf, v_bf):
    t = q_bf.shape[0]
    tq = _tile(t, 128)
    tk = _tile(t, 1024)
    ts = _tile(tk, 256)
    k_sel = min(TOPK_MAX, t // 4)
    whole = lambda shape: pl.BlockSpec(shape, lambda i: (0, 0), pipeline_mode=pl.Buffered(1))
    return pl.pallas_call(
        functools.partial(_dsa_kernel, tq=tq, tk=tk, ts=ts, k_sel=k_sel),
        out_shape=jax.ShapeDtypeStruct((t, DSA_HEADS * DSA_HD), F32),
        grid=(t // tq,),
        in_specs=[pl.BlockSpec((tq, DSA_HEADS * DSA_HD), lambda i: (i, 0)),
                  pl.BlockSpec((tq, IDX_HEADS * IDX_D), lambda i: (i, 0)),
                  pl.BlockSpec((tq, LANE), lambda i: (i, 0)),
                  whole((IDX_D, t)), whole((DSA_KV_HEADS * DSA_HD, t)), whole((t, DSA_KV_HEADS * DSA_HD))],
        out_specs=pl.BlockSpec((tq, DSA_HEADS * DSA_HD), lambda i: (i, 0)),
        scratch_shapes=[pltpu.VMEM((tq, t), I32),
                        pltpu.VMEM((tk, tk), BF16),
                        pltpu.VMEM((DSA_KV_HEADS, DSA_G * tq, DSA_HD), BF16),
                        pltpu.VMEM((IDX_HEADS, tq, LANE), F32),
                        pltpu.VMEM((DSA_KV_HEADS, DSA_G * tq, 1), F32),
                        pltpu.VMEM((DSA_KV_HEADS, DSA_G * tq, 1), F32),
                        pltpu.VMEM((DSA_KV_HEADS, DSA_G * tq, DSA_HD), F32)],
        compiler_params=_cparams(("arbitrary",)),
        name="dsa_prompt",
    )(q_bf, qi_bf, wi, kit_bf, kt_bf, v_bf)


def _sdsa_select_kernel(pt_ref, qi_ref, w_ref, kn_ref, *rest, ppc, ts, k_sel, past):
    pages = rest[:ppc]
    keys_out, tau_out, need_out, tie_out = rest[ppc:ppc + 4]
    key_ref, kc_ref = rest[ppc + 4:]
    c = pl.program_id(1)
    nch = pl.num_programs(1)
    qi = qi_ref[0]
    w = w_ref[0]

    def token_scores(sc):
        x = jnp.maximum(sc, 0.0) * w
        parts = [jnp.sum(x[t * IDX_HEADS:(t + 1) * IDX_HEADS], axis=0, keepdims=True) for t in range(ts)]
        return jnp.concatenate(parts + [jnp.zeros((SUB - ts, sc.shape[1]), F32)], axis=0)

    for j in range(ppc):
        kc_ref[j * PAGE_SIZE:(j + 1) * PAGE_SIZE, :] = pages[j][0, 0].astype(BF16)
    width = ppc * PAGE_SIZE
    score = token_scores(_dot_nt(qi, kc_ref[...]))
    rows = lax.broadcasted_iota(I32, (SUB, width), 0)
    key_ref[:, pl.ds(pl.multiple_of(c * width, width), width)] = jnp.where(rows < ts, _sort_key(score), INT_MIN)

    @pl.when(c == nch - 1)
    def _():
        kn = jnp.concatenate([kn_ref[0], jnp.zeros((LANE - ts, IDX_D), F32)], axis=0).astype(BF16)
        sc = token_scores(_dot_nt(qi, kn))
        r8 = lax.broadcasted_iota(I32, (SUB, LANE), 0)
        l8 = lax.broadcasted_iota(I32, (SUB, LANE), 1)
        ok = (r8 < ts) & (l8 <= r8)
        key_ref[:, past:past + LANE] = jnp.where(ok, _sort_key(sc), INT_MIN)

        def count_ge(cand):
            return jnp.sum(jnp.where(key_ref[...] >= cand, 1.0, 0.0), axis=1, keepdims=True)

        tau, need, tie = _kth_largest_key(count_ge, k_sel)
        keys_out[0] = key_ref[...]
        tau_out[0] = jnp.broadcast_to(tau, (SUB, LANE))
        need_out[0] = jnp.broadcast_to(need, (SUB, LANE))
        tie_out[0] = jnp.broadcast_to(tie, (SUB, LANE))


def _sdsa_attend_kernel(pt_ref, q_ref, keys_ref, ktail_ref, tau_ref, need_ref, tie_ref, kn_ref, vn_ref, *rest,
                        ppc, ts):
    kpages = rest[:ppc]
    vpages = rest[ppc:2 * ppc]
    o_ref = rest[2 * ppc]
    m_ref, l_ref, acc_ref, run_ref, tri_ref, kc_ref, vc_ref = rest[2 * ppc + 1:]
    c = pl.program_id(1)
    nch = pl.num_programs(1)
    width = ppc * PAGE_SIZE
    nq = DSA_KV_HEADS * SQ_ROWS

    @pl.when(c == 0)
    def _():
        m_ref[...] = jnp.full_like(m_ref, NEG_BIG)
        l_ref[...] = jnp.zeros_like(l_ref)
        acc_ref[...] = jnp.zeros_like(acc_ref)
        run_ref[...] = jnp.zeros_like(run_ref)
        r = lax.broadcasted_iota(I32, (width, width), 0)
        cc = lax.broadcasted_iota(I32, (width, width), 1)
        tri_ref[...] = (r <= cc).astype(BF16)

    tau = tau_ref[0][:, 0:1]
    need = need_ref[0][:, 0:1]
    tau_eff = jnp.maximum(tau, INT_MIN + 1)
    has_tie = jnp.max(tie_ref[0]) > 0.0
    q = q_ref[0]

    def process(keys, kb, vb):
        w = keys.shape[1]

        def fast(run_eq):
            return jnp.where(keys >= tau_eff, 0.0, NEG_BIG), run_eq

        def slow(run_eq):
            eq = (keys == tau) & (keys > INT_MIN)
            rank = run_eq + _dot(eq.astype(BF16), tri_ref[0:w, 0:w])
            sel = (keys > tau) | (eq & (rank <= need))
            return jnp.where(sel, 0.0, NEG_BIG), run_eq + jnp.sum(eq.astype(F32), axis=1, keepdims=True)

        bias, run_eq = lax.cond(has_tie, slow, fast, run_ref[...])
        run_ref[...] = run_eq
        bias_all = jnp.concatenate([bias] * (nq // SUB), axis=0)
        s = _dot_nt(q, kb) + bias_all
        m_old = m_ref[...]
        m_new = jnp.maximum(m_old, jnp.max(s, axis=1, keepdims=True))
        p = jnp.exp2(s - m_new)
        alpha = jnp.exp2(m_old - m_new)
        l_ref[...] = alpha * l_ref[...] + jnp.sum(p, axis=1, keepdims=True)
        acc_ref[...] = alpha * acc_ref[...] + _dot(p.astype(BF16), vb)
        m_ref[...] = m_new

    for j in range(ppc):
        for n in range(DSA_KV_HEADS):
            sl = slice(n * DSA_HD, (n + 1) * DSA_HD)
            kc_ref[j * PAGE_SIZE:(j + 1) * PAGE_SIZE, sl] = kpages[j][0, 0, :, n, :].astype(BF16)
            vc_ref[j * PAGE_SIZE:(j + 1) * PAGE_SIZE, sl] = vpages[j][0, 0, :, n, :].astype(BF16)
    process(keys_ref[0], kc_ref[...], vc_ref[...])

    @pl.when(c == nch - 1)
    def _():
        d = DSA_KV_HEADS * DSA_HD
        kn = jnp.concatenate([kn_ref[0], jnp.zeros((LANE - ts, d), F32)], axis=0).astype(BF16)
        vn = jnp.concatenate([vn_ref[0], jnp.zeros((LANE - ts, d), F32)], axis=0).astype(BF16)
        process(ktail_ref[0], kn, vn)
        o_ref[0] = acc_ref[...] / l_ref[...]


def dsa_sample(q_bf, qi_bf, wi, ki_new, kd_new, vd_new, cache_k, cache_v, cache_idx_k, page_table, layer, bs, ts):
    assert ts <= SUB
    n_pages = page_table.shape[1]
    past = n_pages * PAGE_SIZE
    ppc = 8 if n_pages % 8 == 0 else n_pages
    nch = n_pages // ppc
    width = ppc * PAGE_SIZE
    nk = past + LANE
    k_sel = min(TOPK_MAX, (past + ts) // 4)
    dkv = DSA_KV_HEADS * DSA_HD
    pt = page_table.reshape(-1).astype(I32)
    qi_r = qi_bf.reshape(bs, ts * IDX_HEADS, IDX_D)
    w_r = wi[:, :IDX_HEADS].reshape(bs, ts * IDX_HEADS, 1)
    cik = cache_idx_k

    def page_spec(tail, j):
        return pl.BlockSpec((1, 1, PAGE_SIZE) + tail,
                            lambda b, c, ptr, j=j: (layer, ptr[b * n_pages + c * ppc + j]) + (0,) * (1 + len(tail)))

    per_b = lambda shape: pl.BlockSpec((1,) + shape, lambda b, c, ptr: (b,) + tuple(0 for _ in shape))
    sel_spec = pltpu.PrefetchScalarGridSpec(
        num_scalar_prefetch=1, grid=(bs, nch),
        in_specs=[per_b((ts * IDX_HEADS, IDX_D)), per_b((ts * IDX_HEADS, 1)), per_b((ts, IDX_D))]
                 + [page_spec((IDX_D,), j) for j in range(ppc)],
        out_specs=[per_b((SUB, nk)), per_b((SUB, LANE)), per_b((SUB, LANE)), per_b((SUB, LANE))],
        scratch_shapes=[pltpu.VMEM((SUB, nk), I32), pltpu.VMEM((width, IDX_D), BF16)])
    keys, tau, need, tie = pl.pallas_call(
        functools.partial(_sdsa_select_kernel, ppc=ppc, ts=ts, k_sel=k_sel, past=past),
        out_shape=[jax.ShapeDtypeStruct((bs, SUB, nk), I32), jax.ShapeDtypeStruct((bs, SUB, LANE), I32),
                   jax.ShapeDtypeStruct((bs, SUB, LANE), F32), jax.ShapeDtypeStruct((bs, SUB, LANE), F32)],
        grid_spec=sel_spec,
        compiler_params=_cparams(("parallel", "arbitrary")),
        name="dsa_sample_select",
    )(pt, qi_r, w_r, ki_new.reshape(bs, ts, IDX_D), *([cik] * ppc))

    qx = q_bf.reshape(bs, ts, DSA_KV_HEADS, DSA_G, DSA_HD).transpose(0, 2, 3, 1, 4)
    qx = jnp.pad(qx, ((0, 0), (0, 0), (0, 0), (0, SUB - ts), (0, 0))).reshape(bs, DSA_KV_HEADS, SQ_ROWS, DSA_HD)
    eye = jnp.eye(DSA_KV_HEADS, dtype=BF16)
    q_bd = (qx[:, :, :, None, :] * eye[None, :, None, :, None]).reshape(bs, DSA_KV_HEADS * SQ_ROWS, dkv)
    nq = DSA_KV_HEADS * SQ_ROWS
    att_spec = pltpu.PrefetchScalarGridSpec(
        num_scalar_prefetch=1, grid=(bs, nch),
        in_specs=[per_b((nq, dkv)),
                  pl.BlockSpec((1, SUB, width), lambda b, c, ptr: (b, 0, c)),
                  pl.BlockSpec((1, SUB, LANE), lambda b, c, ptr: (b, 0, past // LANE)),
                  per_b((SUB, LANE)), per_b((SUB, LANE)), per_b((SUB, LANE)),
                  per_b((ts, dkv)), per_b((ts, dkv))]
                 + [page_spec((DSA_KV_HEADS, DSA_HD), j % ppc) for j in range(2 * ppc)],
        out_specs=per_b((nq, dkv)),
        scratch_shapes=[pltpu.VMEM((nq, 1), F32), pltpu.VMEM((nq, 1), F32), pltpu.VMEM((nq, dkv), F32),
                        pltpu.VMEM((SUB, 1), F32), pltpu.VMEM((width, width), BF16),
                        pltpu.VMEM((width, dkv), BF16), pltpu.VMEM((width, dkv), BF16)])
    o_bd = pl.pallas_call(
        functools.partial(_sdsa_attend_kernel, ppc=ppc, ts=ts),
        out_shape=jax.ShapeDtypeStruct((bs, nq, dkv), F32),
        grid_spec=att_spec,
        compiler_params=_cparams(("parallel", "arbitrary")),
        name="dsa_sample_attend",
    )(pt, q_bd, keys, keys, tau, need, tie, kd_new.reshape(bs, ts, dkv), vd_new.reshape(bs, ts, dkv),
      *([cache_k] * ppc), *([cache_v] * ppc))
    o6 = o_bd.reshape(bs, DSA_KV_HEADS, DSA_G, SUB, DSA_KV_HEADS, DSA_HD)
    ar = jnp.arange(DSA_KV_HEADS)
    od = o6[:, ar, :, :, ar, :]
    return od[:, :, :, :ts].transpose(1, 3, 0, 2, 4).reshape(bs * ts, DSA_HEADS * DSA_HD)


def _gelu_tanh(x):
    return 0.5 * x * (1.0 + jnp.tanh(math.sqrt(2.0 / math.pi) * (x + 0.044715 * x * x * x)))


def _softplus(x):
    return jnp.maximum(x, 0.0) + jnp.log1p(jnp.exp(-jnp.abs(x)))


def _lru_gates(xc, wa, ba, wx, bx, lam):
    xcb = xc.astype(BF16)
    gate_r = _sigmoid(_dot(xcb, wa) + ba)
    gate_i = _sigmoid(_dot(xcb, wx) + bx)
    log_a = -LRU_C * gate_r * _softplus(-lam)
    th = jnp.tanh(log_a)
    mult = jnp.sqrt(-2.0 * th / (1.0 - th))
    return jnp.exp(log_a), mult, gate_i * xc


def _lru_kernel(x_ref, y_ref, cw_ref, cb_ref, wa_ref, ba_ref, wx_ref, bx_ref, lam_ref,
                o_ref, hl_ref, xp_ref, h_ref, *, tb):
    i = pl.program_id(0)

    @pl.when(i == 0)
    def _():
        xp_ref[0:8, :] = jnp.zeros((8, LRU_W), F32)
        h_ref[...] = jnp.zeros_like(h_ref)

    xp_ref[8:8 + tb, :] = x_ref[...]
    rows = lax.broadcasted_iota(I32, (tb, LRU_BW), 0)
    first = (rows == 0) & (i == 0)
    for n in range(LRU_BLOCKS):
        sl = slice(n * LRU_BW, (n + 1) * LRU_BW)
        xc = jnp.broadcast_to(cb_ref[:, sl], (tb, LRU_BW))
        for w in range(CONV_W):
            xc = xc + xp_ref[8 - (CONV_W - 1) + w:8 - (CONV_W - 1) + w + tb, sl] * cw_ref[w:w + 1, sl]
        a, mult, gx = _lru_gates(xc, wa_ref[n], ba_ref[:, sl], wx_ref[n], bx_ref[:, sl], lam_ref[:, sl])
        b = jnp.where(first, 1.0, mult) * gx
        d = 1
        while d < tb:
            a_sh = pltpu.roll(a, d, 0)
            b_sh = pltpu.roll(b, d, 0)
            ok = rows >= d
            b = jnp.where(ok, a * b_sh + b, b)
            a = jnp.where(ok, a * a_sh, a)
            d *= 2
        h = a * h_ref[:, sl] + b
        h_ref[:, sl] = h[tb - 1:tb, :]
        o_ref[:, sl] = h * _gelu_tanh(y_ref[:, sl])
    xp_ref[0:8, :] = xp_ref[tb:tb + 8, :]
    hl_ref[...] = h_ref[...]


def rglru_prompt(z, conv_w, conv_b, wa, ba, wx, bx, lam):
    t = z.shape[0]
    tb = _tile(t, 256)
    vec = lambda a: a.reshape(1, LRU_W)
    return pl.pallas_call(
        functools.partial(_lru_kernel, tb=tb),
        out_shape=[jax.ShapeDtypeStruct((t, LRU_W), F32), jax.ShapeDtypeStruct((1, LRU_W), F32)],
        grid=(t // tb,),
        in_specs=[pl.BlockSpec((tb, LRU_W), lambda i: (i, Z_LX // LRU_W)),
                  pl.BlockSpec((tb, LRU_W), lambda i: (i, Z_LY // LRU_W)),
                  _cst((CONV_W, LRU_W)), _cst((1, LRU_W)),
                  _cst((LRU_BLOCKS, LRU_BW, LRU_BW)), _cst((1, LRU_W)),
                  _cst((LRU_BLOCKS, LRU_BW, LRU_BW)), _cst((1, LRU_W)), _cst((1, LRU_W))],
        out_specs=[pl.BlockSpec((tb, LRU_W), lambda i: (i, 0)), _cst((1, LRU_W))],
        scratch_shapes=[pltpu.VMEM((tb + 8, LRU_W), F32), pltpu.VMEM((1, LRU_W), F32)],
        compiler_params=_cparams(("arbitrary",)),
        name="rglru_prompt",
    )(z, z, conv_w, vec(conv_b), wa.astype(BF16), vec(ba), wx.astype(BF16), vec(bx), vec(lam))


def _lru_sample_kernel(xs_ref, y_ref, h0_ref, cw_ref, cb_ref, wa_ref, ba_ref, wx_ref, bx_ref, lam_ref,
                       o_ref, hl_ref, *, ts, first_is_start):
    h = h0_ref[0]
    for t in range(ts):
        xc = cb_ref[...]
        for w in range(CONV_W):
            xc = xc + xs_ref[t + w] * cw_ref[w:w + 1, :]
        a, mult, gx = _lru_gates(xc, wa_ref[0], ba_ref[...], wx_ref[0], bx_ref[...], lam_ref[...])
        if first_is_start and t == 0:
            mult = jnp.ones_like(mult)
        h = a * h + mult * gx
        o_ref[t] = h * _gelu_tanh(y_ref[t])
    hl_ref[...] = h


def rglru_sample(xs, ly, state_lru, layer, conv_w, conv_b, wa, ba, wx, bx, lam, first_is_start):
    ts, b, w = ly.shape
    vec = lambda a: a.reshape(1, w)
    blk = lambda r: pl.BlockSpec((r, b, LRU_BW), lambda n: (0, 0, n))
    col = lambda r: pl.BlockSpec((r, LRU_BW), lambda n: (0, n))
    wsp = pl.BlockSpec((1, LRU_BW, LRU_BW), lambda n: (n, 0, 0))
    return pl.pallas_call(
        functools.partial(_lru_sample_kernel, ts=ts, first_is_start=first_is_start),
        out_shape=[jax.ShapeDtypeStruct((ts, b, w), F32), jax.ShapeDtypeStruct((b, w), F32)],
        grid=(LRU_BLOCKS,),
        in_specs=[blk(CONV_W - 1 + ts), blk(ts),
                  pl.BlockSpec((1, b, LRU_BW), lambda n: (layer, 0, n)),
                  col(CONV_W), col(1), wsp, col(1), wsp, col(1), col(1)],
        out_specs=[blk(ts), pl.BlockSpec((b, LRU_BW), lambda n: (0, n))],
        compiler_params=_cparams(("parallel",)),
        name="rglru_sample",
    )(xs, ly, state_lru, conv_w, vec(conv_b), wa.astype(BF16), vec(ba), wx.astype(BF16), vec(bx), vec(lam))


def _gla_gate_kernel(glr_ref, w2_ref, bg_ref, o_ref):
    x = _dot(glr_ref[...].astype(BF16), w2_ref[...]) + bg_ref[...]
    o_ref[...] = _log_sigmoid(x) * (1.0 / GLA_TAU)


def gla_gate(z, w2, bg):
    t = z.shape[0]
    w = GLA_HEADS * GLA_DK
    w2p = jnp.pad(w2, ((0, LANE - GLA_RANK), (0, 0))).astype(BF16)
    return pl.pallas_call(
        _gla_gate_kernel,
        out_shape=jax.ShapeDtypeStruct((t, w), F32),
        grid=(1,),
        in_specs=[pl.BlockSpec((t, LANE), lambda i: (0, Z_GLR // LANE)), _cst((LANE, w)), _cst((1, w))],
        out_specs=_cst((t, w)),
        compiler_params=_cparams(("arbitrary",)),
        name="gla_gate",
    )(z, w2p, bg.reshape(1, w))


def _gla_sample_kernel(qt_ref, kt_ref, gt_ref, v_ref, r_ref, s0_ref, ng_ref, o_ref, s_ref, *, ts):
    ng = ng_ref[...]
    for h in range(GLA_HEADS):
        s = s0_ref[0, 0, h]
        qt = qt_ref[0, h] * (GLA_DK ** -0.5)
        kt = kt_ref[0, h]
        dec = jnp.exp(gt_ref[0, h])
        sl = slice(h * GLA_DV, (h + 1) * GLA_DV)
        for t in range(ts):
            s = dec[:, t:t + 1] * s + kt[:, t:t + 1] * v_ref[0, t:t + 1, sl]
            o = jnp.sum(qt[:, t:t + 1] * s, axis=0, keepdims=True)
            on = o * lax.rsqrt(jnp.mean(o * o, -1, keepdims=True) + LN_EPS) * ng
            o_ref[0, t:t + 1, sl] = on * _silu(r_ref[0, t:t + 1, sl])
        s_ref[0, h] = s


def gla_sample(qt, kt, gt, v, r, state_gla, layer, ng):
    b, h, dk, ts = qt.shape
    col = pl.BlockSpec((1, h, dk, ts), lambda i: (i, 0, 0, 0))
    row = pl.BlockSpec((1, ts, h * GLA_DV), lambda i: (i, 0, 0))
    return pl.pallas_call(
        functools.partial(_gla_sample_kernel, ts=ts),
        out_shape=[jax.ShapeDtypeStruct((b, ts, h * GLA_DV), F32),
                   jax.ShapeDtypeStruct((b, h, dk, GLA_DV), F32)],
        grid=(b,),
        in_specs=[col, col, col, row, row,
                  pl.BlockSpec((1, 1, h, dk, GLA_DV), lambda i: (layer, i, 0, 0, 0)), _cst((1, GLA_DV))],
        out_specs=[row, pl.BlockSpec((1, h, dk, GLA_DV), lambda i: (i, 0, 0, 0))],
        compiler_params=_cparams(("parallel",)),
        name="gla_sample",
    )(qt, kt, gt, v, r, state_gla, ng.reshape(1, GLA_DV))


def _mem_heads(q_of, k_of, v_of, store):
    scale = MEM_HD ** -0.5
    for h in range(MEM_HEADS):
        sl = slice(h * MEM_HD, (h + 1) * MEM_HD)
        s = _dot_nt(q_of(sl).astype(BF16), k_of(sl).astype(BF16)) * scale
        p = jnp.exp(s - jnp.max(s, -1, keepdims=True))
        p = p / jnp.sum(p, -1, keepdims=True)
        store(sl, _dot(p.astype(BF16), v_of(sl).astype(BF16)))


def _mem_kernel(q_ref, mk_ref, mv_ref, o_ref):
    def store(sl, val):
        o_ref[:, sl] = val
    _mem_heads(lambda sl: q_ref[:, sl], lambda sl: mk_ref[:, sl], lambda sl: mv_ref[:, sl], store)


def mem_attend_prompt(z, mkv):
    t = z.shape[0]
    tb = _tile(t, 512)
    w = MEM_HEADS * MEM_HD
    return pl.pallas_call(
        _mem_kernel,
        out_shape=jax.ShapeDtypeStruct((t, w), F32),
        grid=(t // tb,),
        in_specs=[pl.BlockSpec((tb, w), lambda i: (i, Z_MQ // w)),
                  pl.BlockSpec((N_MEM, w), lambda i: (0, 0)),
                  pl.BlockSpec((N_MEM, w), lambda i: (0, 1))],
        out_specs=pl.BlockSpec((tb, w), lambda i: (i, 0)),
        compiler_params=_cparams(("parallel",)),
        name="mem_attend",
    )(z, mkv, mkv)


def _mem_sample_kernel(q_ref, mk_ref, mv_ref, o_ref):
    def store(sl, val):
        o_ref[0, :, sl] = val
    head = lambda sl: sl.start // MEM_HD
    _mem_heads(lambda sl: q_ref[0, :, sl], lambda sl: mk_ref[0, 0, :, head(sl), :],
               lambda sl: mv_ref[0, 0, :, head(sl), :], store)


def mem_attend_sample(q, mem_k, mem_v, layer):
    b, rows, w = q.shape
    qs = pl.BlockSpec((1, rows, w), lambda i: (i, 0, 0))
    ms = pl.BlockSpec((1, 1, N_MEM, MEM_HEADS, MEM_HD), lambda i: (layer, i, 0, 0, 0))
    return pl.pallas_call(
        _mem_sample_kernel,
        out_shape=jax.ShapeDtypeStruct((b, rows, w), F32),
        grid=(b,),
        in_specs=[qs, ms, ms],
        out_specs=qs,
        compiler_params=_cparams(("parallel",)),
        name="mem_attend_sample",
    )(q, mem_k, mem_v)


def _merge_kernel(b0_ref, b1_ref, b2_ref, b3_ref, g0_ref, g1_ref, g2_ref, g3_ref, x_ref,
                  wb_ref, bg_ref, wo_ref, lg_ref, lb_ref, o_ref, *, alpha):
    brs = (b0_ref, b1_ref, b2_ref, b3_ref)
    gls = (g0_ref, g1_ref, g2_ref, g3_ref)
    merged = None
    for j in range(N_BRANCH):
        proj = _dot(brs[j][...].astype(BF16), wb_ref[j])
        term = _sigmoid(gls[j][...] + bg_ref[j:j + 1, :]) * proj
        merged = term if merged is None else merged + term
    y = _dot(merged.astype(BF16), wo_ref[...])
    o_ref[...] = _ln_rows(alpha * x_ref[...] + y, lg_ref[...], lb_ref[...])


def merge_out(branches, z, x, wb, bgate, wo, lg, lb, alpha):
    t = x.shape[0]
    tb = _tile(t, 256)
    d = D_MODEL
    row = pl.BlockSpec((tb, d), lambda i: (i, 0))
    gl = [pl.BlockSpec((tb, d), lambda i, o=Z_GL // d + j: (i, o)) for j in range(N_BRANCH)]
    return pl.pallas_call(
        functools.partial(_merge_kernel, alpha=alpha),
        out_shape=jax.ShapeDtypeStruct((t, d), F32),
        grid=(t // tb,),
        in_specs=[row, row, row, row] + gl + [row, _cst((N_BRANCH, d, d)), _cst((N_BRANCH, d)), _cst((d, d)),
                                               _cst((1, d)), _cst((1, d))],
        out_specs=row,
        compiler_params=_cparams(("parallel",)),
        name="merge_out",
    )(*branches, z, z, z, z, x, wb, bgate, wo, lg.reshape(1, d), lb.reshape(1, d))


def _first_argmax(x, n):
    m = jnp.max(x, axis=0, keepdims=True)
    ri = lax.broadcasted_iota(I32, x.shape, 0)
    idx = jnp.min(jnp.where(x == m, ri, n), axis=0, keepdims=True)
    return m, idx, ri


def _router_kernel(x_ref, rwt_ref, rb_ref, e_ref, w_ref, r_ref, c_ref, cnt_ref, tri_ref):
    i = pl.program_id(0)

    @pl.when(i == 0)
    def _():
        cnt_ref[...] = jnp.zeros_like(cnt_ref)
        r = lax.broadcasted_iota(I32, tri_ref.shape, 0)
        c = lax.broadcasted_iota(I32, tri_ref.shape, 1)
        tri_ref[...] = (r < c).astype(BF16)

    logits = lax.dot_general(rwt_ref[...], x_ref[...], (((1,), (1,)), ((), ())),
                             preferred_element_type=F32, precision=lax.Precision.HIGHEST)
    s = _sigmoid(logits)
    sb = s + rb_ref[...]
    gsz = N_EXPERTS // N_GROUPS
    neg = -jnp.inf
    gs = []
    for g in range(N_GROUPS):
        blk = sb[g * gsz:(g + 1) * gsz, :]
        m1, i1, ri = _first_argmax(blk, gsz)
        m2 = jnp.max(jnp.where(ri == i1, neg, blk), axis=0, keepdims=True)
        gs.append(m1 + m2)
    gscore = jnp.concatenate(gs, axis=0)
    gmask = jnp.zeros(gscore.shape, jnp.bool_)
    for _ in range(TOPK_GROUPS):
        _, gi, ri = _first_argmax(gscore, N_GROUPS)
        hit = ri == gi
        gmask = gmask | hit
        gscore = jnp.where(hit, neg, gscore)
    cand = jnp.concatenate(
        [jnp.where(gmask[g:g + 1, :], sb[g * gsz:(g + 1) * gsz, :], neg) for g in range(N_GROUPS)], axis=0)
    idxs, ws, ranks = [], [], []
    run = cnt_ref[...]
    for _ in range(TOP_K):
        _, ei, ri = _first_argmax(cand, N_EXPERTS)
        hit = ri == ei
        idxs.append(ei)
        ws.append(jnp.sum(jnp.where(hit, s, 0.0), axis=0, keepdims=True))
        cand = jnp.where(hit, neg, cand)
        hb = hit.astype(BF16)
        before = run + _dot(hb, tri_ref[...])
        ranks.append(jnp.sum(jnp.where(hit, before, 0.0), axis=0, keepdims=True))
        run = run + jnp.sum(hb.astype(F32), axis=1, keepdims=True)
    cnt_ref[...] = run
    c_ref[...] = run
    w = jnp.concatenate(ws, axis=0)
    e_ref[...] = jnp.concatenate(idxs, axis=0)
    w_ref[...] = w / jnp.sum(w, axis=0, keepdims=True) * ROUTED_SCALE
    r_ref[...] = jnp.concatenate(ranks, axis=0).astype(I32)


def moe_router(x, router_w, router_b):
    t, d = x.shape
    tb = _tile(t, 512)
    if tb % LANE != 0:
        tb = t
    tok = pl.BlockSpec((TOP_K, tb), lambda i: (0, i))
    return pl.pallas_call(
        _router_kernel,
        out_shape=[jax.ShapeDtypeStruct((TOP_K, t), I32), jax.ShapeDtypeStruct((TOP_K, t), F32),
                   jax.ShapeDtypeStruct((TOP_K, t), I32), jax.ShapeDtypeStruct((N_EXPERTS, 1), F32)],
        grid=(t // tb,),
        in_specs=[pl.BlockSpec((tb, d), lambda i: (i, 0)), _cst((N_EXPERTS, d)), _cst((N_EXPERTS, 1))],
        out_specs=[tok, tok, tok, _cst((N_EXPERTS, 1))],
        scratch_shapes=[pltpu.VMEM((N_EXPERTS, 1), F32), pltpu.VMEM((tb, tb), BF16)],
        compiler_params=_cparams(("arbitrary",)),
        name="moe_router",
    )(x, router_w.T, router_b.reshape(N_EXPERTS, 1))


def _dest_kernel(e_ref, r_ref, ps_ref, d_ref):
    ps = ps_ref[...]
    rows = []
    for k in range(TOP_K):
        e = e_ref[k:k + 1, :]
        ri = lax.broadcasted_iota(I32, (N_EXPERTS, e.shape[1]), 0)
        rows.append(jnp.sum(jnp.where(ri == e, ps, 0), axis=0, keepdims=True))
    d_ref[...] = jnp.concatenate(rows, axis=0) + r_ref[...]


def moe_dest(eidx, rank, pstarts):
    k, t = eidx.shape
    tb = _tile(t, 512)
    if tb % LANE != 0:
        tb = t
    tok = pl.BlockSpec((k, tb), lambda i: (0, i))
    return pl.pallas_call(
        _dest_kernel,
        out_shape=jax.ShapeDtypeStruct((k, t), I32),
        grid=(t // tb,),
        in_specs=[tok, tok, _cst((N_EXPERTS, 1))],
        out_specs=tok,
        compiler_params=_cparams(("parallel",)),
        name="moe_dest",
    )(eidx, rank, pstarts.reshape(N_EXPERTS, 1).astype(I32))


def _dispatch_kernel(dest_ref, x_ref, xp_in, xp_out, sem, *, tb):
    del xp_in

    def issue(r, c):
        for k in range(TOP_K):
            pltpu.make_async_copy(x_ref.at[r], xp_out.at[dest_ref[r * TOP_K + k]], sem).start()
        return c

    lax.fori_loop(0, tb, issue, 0)
    for k in range(TOP_K):
        pltpu.make_async_copy(x_ref, xp_out.at[pl.ds(0, tb)], sem).wait()


def _token_tile(t):
    tb = LANE
    while t % tb:
        tb //= 2
    assert tb * TOP_K >= LANE, "token count must be a multiple of 16"
    return tb


def moe_dispatch(x3, dest_flat, p):
    t = x3.shape[0]
    tb = _token_tile(t)
    return pl.pallas_call(
        functools.partial(_dispatch_kernel, tb=tb),
        out_shape=jax.ShapeDtypeStruct((p, SUB, LANE), F32),
        grid=(t // tb,),
        in_specs=[pl.BlockSpec((tb * TOP_K,), lambda i: (i,), memory_space=pltpu.SMEM),
                  pl.BlockSpec((tb, SUB, LANE), lambda i: (i, 0, 0)),
                  pl.BlockSpec(memory_space=pl.ANY)],
        out_specs=pl.BlockSpec(memory_space=pl.ANY),
        scratch_shapes=[pltpu.SemaphoreType.DMA(())],
        input_output_aliases={2: 0},
        compiler_params=_cparams(("arbitrary",)),
        name="moe_dispatch",
    )(dest_flat, x3, jnp.zeros((p, SUB, LANE), F32))


def _experts_kernel(be_ref, nv_ref, x_ref, w1_ref, w3_ref, w2_ref, o_ref, w1b, w3b, w2b, xs_ref):
    i = pl.program_id(0)
    e = be_ref[i]
    prev = be_ref[jnp.maximum(i - 1, 0)]

    @pl.when((i == 0) | (e != prev))
    def _():
        w1b[...] = w1_ref[0].astype(BF16)
        w3b[...] = w3_ref[0].astype(BF16)
        w2b[...] = w2_ref[0].astype(BF16)

    @pl.when(i < nv_ref[0])
    def _():
        for s in range(SUB):
            xs_ref[:, s * LANE:(s + 1) * LANE] = x_ref[:, s, :].astype(BF16)
        x = xs_ref[...]
        h = _silu(_dot(x, w1b[...])) * _dot(x, w3b[...])
        y = _dot(h.astype(BF16), w2b[...])
        for s in range(SUB):
            o_ref[:, s, :] = y[:, s * LANE:(s + 1) * LANE]

    @pl.when(i >= nv_ref[0])
    def _():
        o_ref[...] = jnp.zeros_like(o_ref)


def moe_experts(xp, block_e, n_valid, w1, w3, w2):
    p = xp.shape[0]
    d, de = w1.shape[1], w1.shape[2]
    nb = p // MOE_BLK
    tile = pl.BlockSpec((MOE_BLK, SUB, LANE), lambda i, be, nv: (i, 0, 0))
    grid_spec = pltpu.PrefetchScalarGridSpec(
        num_scalar_prefetch=2,
        grid=(nb,),
        in_specs=[tile,
                  pl.BlockSpec((1, d, de), lambda i, be, nv: (be[i], 0, 0)),
                  pl.BlockSpec((1, d, de), lambda i, be, nv: (be[i], 0, 0)),
                  pl.BlockSpec((1, de, d), lambda i, be, nv: (be[i], 0, 0))],
        out_specs=tile,
        scratch_shapes=[pltpu.VMEM((d, de), BF16), pltpu.VMEM((d, de), BF16), pltpu.VMEM((de, d), BF16),
                        pltpu.VMEM((MOE_BLK, d), BF16)])
    return pl.pallas_call(
        _experts_kernel,
        out_shape=jax.ShapeDtypeStruct((p, SUB, LANE), F32),
        grid_spec=grid_spec,
        compiler_params=_cparams(("arbitrary",)),
        name="moe_experts",
    )(block_e, n_valid, xp, w1, w3, w2)


def _moe_final_kernel(dest_ref, x_ref, w_ref, yp_hbm, w1_ref, w3_ref, w2_ref, lg_ref, lb_ref, o_ref,
                      buf, r_ref, sem, *, alpha, tb):
    def issue(r, c):
        for k in range(TOP_K):
            pltpu.make_async_copy(yp_hbm.at[dest_ref[r * TOP_K + k]], buf.at[k, r], sem).start()
        return c

    lax.fori_loop(0, tb, issue, 0)
    for k in range(TOP_K):
        pltpu.make_async_copy(yp_hbm.at[pl.ds(0, tb)], buf.at[k], sem).wait()
    w = w_ref[...]
    wk = [jnp.broadcast_to(w[:, k:k + 1], (tb, LANE)) for k in range(TOP_K)]
    for s in range(SUB):
        acc = buf[0, :, s, :] * wk[0]
        for k in range(1, TOP_K):
            acc = acc + buf[k, :, s, :] * wk[k]
        r_ref[:, s * LANE:(s + 1) * LANE] = acc
    x = x_ref[...]
    xb = x.astype(BF16)
    h = _silu(_dot(xb, w1_ref[...])) * _dot(xb, w3_ref[...])
    shared = _dot(h.astype(BF16), w2_ref[...])
    o_ref[...] = _ln_rows(alpha * x + (r_ref[...] + shared), lg_ref[...], lb_ref[...])


def moe_final(x, wsel, dest_flat, yp, w1, w3, w2, lg, lb, alpha):
    t, d = x.shape
    tb = _token_tile(t)
    ds = w1.shape[1]
    row = pl.BlockSpec((tb, d), lambda i: (i, 0))
    return pl.pallas_call(
        functools.partial(_moe_final_kernel, alpha=alpha, tb=tb),
        out_shape=jax.ShapeDtypeStruct((t, d), F32),
        grid=(t // tb,),
        in_specs=[pl.BlockSpec((tb * TOP_K,), lambda i: (i,), memory_space=pltpu.SMEM),
                  row, pl.BlockSpec((tb, TOP_K), lambda i: (i, 0)), pl.BlockSpec(memory_space=pl.ANY),
                  _cst((d, ds)), _cst((d, ds)), _cst((ds, d)), _cst((1, d)), _cst((1, d))],
        out_specs=row,
        scratch_shapes=[pltpu.VMEM((TOP_K, tb, SUB, LANE), F32), pltpu.VMEM((tb, d), F32),
                        pltpu.SemaphoreType.DMA(())],
        compiler_params=_cparams(("arbitrary",)),
        name="moe_final",
    )(dest_flat, x, wsel, yp, w1.astype(BF16), w3.astype(BF16), w2.astype(BF16), lg.reshape(1, d), lb.reshape(1, d))


def moe_layer(x, lw, alpha):
    n, d = x.shape
    assert d == SUB * LANE, "a token row is moved as one (8,128) tile"
    eidx_t, wsel_t, rank_t, counts = moe_router(x, lw['router_w'], lw['router_bias'])
    m = n * TOP_K
    nb = (m + N_EXPERTS * (MOE_BLK - 1) + MOE_BLK - 1) // MOE_BLK
    p = nb * MOE_BLK
    counts = counts.reshape(N_EXPERTS).astype(I32)
    pcounts = (counts + MOE_BLK - 1) // MOE_BLK * MOE_BLK
    pends = jnp.cumsum(pcounts)
    pstarts = pends - pcounts
    block_e = jnp.minimum(jnp.searchsorted(pends, jnp.arange(nb) * MOE_BLK, side='right'), N_EXPERTS - 1).astype(I32)
    n_valid = (pends[-1] // MOE_BLK).astype(I32).reshape(1)
    dest_flat = moe_dest(eidx_t, rank_t, pstarts).T.reshape(m)
    xp = moe_dispatch(x.reshape(n, SUB, LANE), dest_flat, p)
    yp = moe_experts(xp, block_e, n_valid, lw['exp_w1'], lw['exp_w3'], lw['exp_w2'])
    return moe_final(x, wsel_t.T, dest_flat, yp, lw['sh_w1'], lw['sh_w3'], lw['sh_w2'],
                     lw['ln2_g'], lw['ln2_b'], alpha)


def _pack_w_in(w):
    parts, start = [], 0
    for s in IN_SIZES:
        parts.append(w[:, start:start + s])
        start += s
    gq, gk, gv, gr, glr, dq, dk, dv, iq, ik, iw, lx, ly, mq, gl = parts
    d = w.shape[0]
    padc = lambda a, n: jnp.pad(a, ((0, 0), (0, n - a.shape[1])))
    cols = [gq, gk, gv, gr, dq, dk, dv, iq, padc(glr, LANE), padc(jnp.concatenate([ik, iw], 1), LANE),
            jnp.zeros((d, Z_LX - Z_MISC - LANE), w.dtype), lx, ly, mq, gl]
    out = jnp.concatenate(cols, axis=1).astype(BF16)
    assert out.shape[1] == Z_W
    return out


def _sample_mixers(zs, hs_tabs, l, bs, ts, past, cache_k, cache_v, cache_idx_k, cache_mem_k, cache_mem_v,
                   state_gla, state_conv, state_lru, page_table, p):
    n = bs * ts
    q_bf, kd, _, vd, _, qi_bf, ki, _, wi = dsa_prep(zs, hs_tabs, p['idx_ln_g'], p['idx_ln_b'])
    o_dsa = dsa_sample(q_bf, qi_bf, wi, ki, kd, vd, cache_k, cache_v, cache_idx_k, page_table, l, bs, ts)
    glog = gla_gate(zs, p['gla_w_gate2'], p['gla_b_gate'])
    colz = lambda a: a.reshape(bs, ts, GLA_HEADS, GLA_DK).transpose(0, 2, 3, 1)
    qt = colz(zs[:, Z_GQ:Z_GQ + GLA_HEADS * GLA_DK])
    kt = colz(zs[:, Z_GK:Z_GK + GLA_HEADS * GLA_DK])
    gt = colz(glog)
    rowz = lambda off: zs[:, off:off + GLA_HEADS * GLA_DV].reshape(bs, ts, GLA_HEADS * GLA_DV)
    o_gla, s_new = gla_sample(qt, kt, gt, rowz(Z_GV), rowz(Z_GR), state_gla, l, p['gla_norm_g'])
    lx = zs[:, Z_LX:Z_LX + LRU_W].reshape(bs, ts, LRU_W)
    ly = zs[:, Z_LY:Z_LY + LRU_W].reshape(bs, ts, LRU_W)
    xp = jnp.concatenate([state_conv[l], lx], axis=1)
    o_lru, h_new = rglru_sample(xp.transpose(1, 0, 2), ly.transpose(1, 0, 2), state_lru, l,
                                p['conv_w'], p['conv_b'], p['lru_wa'], p['lru_ba'], p['lru_wx'], p['lru_bx'],
                                p['lru_lambda'], first_is_start=(past == 0))
    conv_new = xp[:, xp.shape[1] - (CONV_W - 1):]
    w = MEM_HEADS * MEM_HD
    rows = -(-ts // SQ_ROWS) * SQ_ROWS
    mq = jnp.pad(zs[:, Z_MQ:Z_MQ + w].reshape(bs, ts, w), ((0, 0), (0, rows - ts), (0, 0)))
    o_mem = mem_attend_sample(mq, cache_mem_k, cache_mem_v, l)
    o_mem = o_mem[:, :ts]
    branches = (o_gla.reshape(n, -1), o_dsa, o_lru.transpose(1, 0, 2).reshape(n, -1), o_mem.reshape(n, -1))
    states = (kd.reshape(bs, ts, DSA_KV_HEADS, DSA_HD), vd.reshape(bs, ts, DSA_KV_HEADS, DSA_HD),
              ki.reshape(bs, ts, IDX_D), s_new, conv_new, h_new)
    return branches, states


def kernel(x_prompt, x_sample, cache_k, cache_v, cache_idx_k, cache_mem_k, cache_mem_v, state_gla, state_conv, state_lru, page_table, mem_prompt, ln_in_g, ln_in_b, w_in, b_gate, gla_w_gate2, gla_b_gate, gla_norm_g, idx_ln_g, idx_ln_b, conv_w, conv_b, lru_wa, lru_ba, lru_wx, lru_bx, lru_lambda, mem_w_kv, w_branch, w_out, ln1_g, ln1_b, ln2_g, ln2_b, router_w, router_bias, exp_w1, exp_w3, exp_w2, sh_w1, sh_w3, sh_w2):
    bp, sp_len, d = x_prompt.shape
    assert bp == 1, "prompt group is a single sequence"
    bs, ts, _ = x_sample.shape
    depth = w_in.shape[0]
    alpha = (2 * depth) ** 0.25
    past = page_table.shape[1] * PAGE_SIZE
    tabs_p = rope_tables(jnp.arange(sp_len, dtype=I32))
    tabs_s = rope_tables(jnp.tile(past + jnp.arange(ts, dtype=I32), bs))

    hp = layer_norm_rows(x_prompt.reshape(sp_len, d), ln_in_g, ln_in_b)
    hs = layer_norm_rows(x_sample.reshape(bs * ts, d), ln_in_g, ln_in_b)

    st_p, st_s, mem_new = [], [], []
    for l in range(depth):
        lw = {'ln2_g': ln2_g[l], 'ln2_b': ln2_b[l], 'router_w': router_w[l], 'router_bias': router_bias[l],
              'exp_w1': exp_w1[l], 'exp_w3': exp_w3[l], 'exp_w2': exp_w2[l],
              'sh_w1': sh_w1[l], 'sh_w3': sh_w3[l], 'sh_w2': sh_w2[l]}
        mp = {'gla_w_gate2': gla_w_gate2[l], 'gla_b_gate': gla_b_gate[l], 'gla_norm_g': gla_norm_g[l],
              'idx_ln_g': idx_ln_g[l], 'idx_ln_b': idx_ln_b[l],
              'conv_w': conv_w[l], 'conv_b': conv_b[l], 'lru_wa': lru_wa[l], 'lru_ba': lru_ba[l],
              'lru_wx': lru_wx[l], 'lru_bx': lru_bx[l], 'lru_lambda': lru_lambda[l]}
        w_in_p = _pack_w_in(w_in[l])
        wb = w_branch[l].astype(BF16)
        wo = w_out[l].astype(BF16)

        zp = matmul(hp, w_in_p)
        mkv = matmul(mem_prompt.reshape(N_MEM, d), mem_w_kv[l].astype(BF16))
        q_bf, kd, k_bf, vd, v_bf, qi_bf, ki, ki_bf, wi = dsa_prep(zp, tabs_p, idx_ln_g[l], idx_ln_b[l])
        o_gla, s_gla = gla_prompt(zp, gla_w_gate2[l], gla_b_gate[l], gla_norm_g[l])
        o_dsa = dsa_prompt(q_bf, qi_bf, wi, ki_bf.T, k_bf.T, v_bf)
        o_lru, h_last = rglru_prompt(zp, conv_w[l], conv_b[l], lru_wa[l], lru_ba[l], lru_wx[l], lru_bx[l], lru_lambda[l])
        o_mem = mem_attend_prompt(zp, mkv)
        xp1 = merge_out((o_gla, o_dsa, o_lru, o_mem), zp, hp, wb, b_gate[l], wo, ln1_g[l], ln1_b[l], alpha)
        lx_p = zp[:, Z_LX:Z_LX + LRU_W]
        conv_p = jnp.concatenate([jnp.zeros((CONV_W - 1, LRU_W), F32), lx_p], 0)[-(CONV_W - 1):]
        st_p.append((kd.reshape(1, sp_len, DSA_KV_HEADS, DSA_HD), vd.reshape(1, sp_len, DSA_KV_HEADS, DSA_HD),
                     ki.reshape(1, sp_len, IDX_D), s_gla[None], conv_p[None], h_last.reshape(1, LRU_W)))
        mem_new.append((mkv[:, :MEM_HEADS * MEM_HD].reshape(1, N_MEM, MEM_HEADS, MEM_HD),
                        mkv[:, MEM_HEADS * MEM_HD:].reshape(1, N_MEM, MEM_HEADS, MEM_HD)))

        zs = matmul(hs, w_in_p)
        brs, sts = _sample_mixers(zs, tabs_s, l, bs, ts, past, cache_k, cache_v, cache_idx_k, cache_mem_k,
                                  cache_mem_v, state_gla, state_conv, state_lru, page_table, mp)
        xs1 = merge_out(brs, zs, hs, wb, b_gate[l], wo, ln1_g[l], ln1_b[l], alpha)
        st_s.append(sts)

        x_all = moe_layer(jnp.concatenate([xp1, xs1], 0), lw, alpha)
        hp, hs = x_all[:sp_len], x_all[sp_len:]

    stk = lambda seq, j: jnp.stack([t[j] for t in seq], axis=0)
    k_p, v_p, ik_p, gla_p, conv_p_, lru_p = [stk(st_p, j) for j in range(6)]
    k_s, v_s, ik_s, gla_s, conv_s, lru_s = [stk(st_s, j) for j in range(6)]
    memk_p, memv_p = stk(mem_new, 0), stk(mem_new, 1)
    return (hp.reshape(1, sp_len, d), hs.reshape(bs, ts, d), k_p, v_p, ik_p, gla_p, conv_p_, lru_p, memk_p, memv_p,
            k_s, v_s, ik_s, gla_s, conv_s, lru_s)
```

```python
import functools
import math

import jax
import jax.numpy as jnp
from jax import lax
from jax.experimental import pallas as pl
from jax.experimental.pallas import tpu as pltpu

F32 = jnp.float32
BF16 = jnp.bfloat16
I32 = jnp.int32

D_MODEL = 1024
PAGE_SIZE = 128
N_BRANCH = 4
GLA_HEADS = 4
GLA_DK = 128
GLA_DV = 256
GLA_RANK = 16
GLA_TAU = 16.0
GLA_CHUNK = 64
DSA_HEADS = 8
DSA_KV_HEADS = 4
DSA_HD = 128
DSA_G = DSA_HEADS // DSA_KV_HEADS
IDX_HEADS = 8
IDX_D = 64
TOPK_MAX = 256
LRU_W = 1024
LRU_BLOCKS = 4
LRU_BW = LRU_W // LRU_BLOCKS
CONV_W = 4
LRU_C = 8.0
N_MEM = 256
MEM_HEADS = 4
MEM_HD = 256
N_EXPERTS = 256
TOP_K = 8
N_GROUPS = 8
TOPK_GROUPS = 4
ROUTED_SCALE = 2.5
ROPE_THETA = 10000.0
LN_EPS = 1e-5

IN_SIZES = (GLA_HEADS * GLA_DK, GLA_HEADS * GLA_DK, GLA_HEADS * GLA_DV, GLA_HEADS * GLA_DV, GLA_RANK,
            DSA_HEADS * DSA_HD, DSA_KV_HEADS * DSA_HD, DSA_KV_HEADS * DSA_HD,
            IDX_HEADS * IDX_D, IDX_D, IDX_HEADS,
            LRU_W, LRU_W, MEM_HEADS * MEM_HD, N_BRANCH * D_MODEL)

LANE = 128
SUB = 8
Z_GQ, Z_GK, Z_GV, Z_GR = 0, 512, 1024, 2048
Z_DQ, Z_DK, Z_DV, Z_IQ = 3072, 4096, 4608, 5120
Z_GLR, Z_MISC = 5632, 5760
Z_LX, Z_LY, Z_MQ, Z_GL = 6144, 7168, 8192, 9216
Z_W = 13312

VMEM_LIMIT = 56 * 1024 * 1024
MOE_BLK = 256
INT_MIN = -2 ** 31
NEG_BIG = -1e30
LOG2E = 1.4426950408889634
SQ_ROWS = 2 * SUB


def _tile(n, target):
    if n <= target:
        return n
    for t in range(target, 7, -1):
        if n % t == 0 and t % 8 == 0:
            return t
    return n


def _cparams(sem, vmem=None):
    return pltpu.CompilerParams(dimension_semantics=sem, vmem_limit_bytes=vmem or VMEM_LIMIT)


def _ln_rows(x, g, b):
    mu = jnp.mean(x, -1, keepdims=True)
    xc = x - mu
    var = jnp.mean(xc * xc, -1, keepdims=True)
    return xc * lax.rsqrt(var + LN_EPS) * g + b


def _sigmoid(x):
    return 1.0 / (1.0 + jnp.exp(-x))


def _silu(x):
    return x * _sigmoid(x)


def _dot(a, b):
    return jnp.dot(a, b, preferred_element_type=F32)


def _dot_nt(a, b):
    return lax.dot_general(a, b, (((1,), (1,)), ((), ())), preferred_element_type=F32)


def _dot_tn(a, b):
    return lax.dot_general(a, b, (((0,), (0,)), ((), ())), preferred_element_type=F32)


def _cst(shape):
    return pl.BlockSpec(shape, lambda *_: tuple(0 for _ in shape))


def _ln_kernel(x_ref, g_ref, b_ref, o_ref):
    o_ref[...] = _ln_rows(x_ref[...], g_ref[...], b_ref[...])


def layer_norm_rows(x, g, b):
    m, d = x.shape
    tm = _tile(m, 512)
    return pl.pallas_call(
        _ln_kernel,
        out_shape=jax.ShapeDtypeStruct((m, d), F32),
        grid=(m // tm,),
        in_specs=[pl.BlockSpec((tm, d), lambda i: (i, 0)), _cst((1, d)), _cst((1, d))],
        out_specs=pl.BlockSpec((tm, d), lambda i: (i, 0)),
        compiler_params=_cparams(("parallel",)),
        name="ln_rows",
    )(x, g.reshape(1, d), b.reshape(1, d))


def _mm_kernel(x_ref, w_ref, o_ref, xb_ref):
    @pl.when(pl.program_id(1) == 0)
    def _():
        xb_ref[...] = x_ref[...].astype(BF16)

    o_ref[...] = _dot(xb_ref[...], w_ref[...])


def matmul(x, w):
    m, k = x.shape
    n = w.shape[1]
    tm = _tile(m, 1024)
    tn = _tile(n, 1024)
    return pl.pallas_call(
        _mm_kernel,
        out_shape=jax.ShapeDtypeStruct((m, n), F32),
        grid=(m // tm, n // tn),
        in_specs=[pl.BlockSpec((tm, k), lambda i, j: (i, 0)),
                  pl.BlockSpec((k, tn), lambda i, j: (0, j))],
        out_specs=pl.BlockSpec((tm, tn), lambda i, j: (i, j)),
        scratch_shapes=[pltpu.VMEM((tm, k), BF16)],
        compiler_params=_cparams(("parallel", "arbitrary")),
        name="matmul",
    )(x, w)


def _rot_half(x, width):
    if width == LANE:
        return pltpu.roll(x, LANE // 2, 1)
    half = width // 2
    lane = lax.broadcasted_iota(I32, x.shape, 1)
    first = (lane % width) < half
    return jnp.where(first, pltpu.roll(x, LANE - half, 1), pltpu.roll(x, half, 1))


def _prep_kernel(dq_ref, dk_ref, dv_ref, iq_ref, misc_ref, c128_ref, s128_ref, c64_ref, s64_ref,
                 ig_ref, ib_ref,
                 q_ref, k_ref, kb_ref, v_ref, vb_ref, qi_ref, ki_ref, kib_ref, wi_ref):
    c128, s128 = c128_ref[...], s128_ref[...]
    c64, s64 = c64_ref[...], s64_ref[...]
    qscale = (DSA_HD ** -0.5) * LOG2E
    for h in range(DSA_HEADS):
        sl = slice(h * LANE, (h + 1) * LANE)
        x = dq_ref[:, sl]
        q_ref[:, sl] = ((x * c128 + _rot_half(x, LANE) * s128) * qscale).astype(BF16)
    for h in range(DSA_KV_HEADS):
        sl = slice(h * LANE, (h + 1) * LANE)
        x = dk_ref[:, sl]
        r = x * c128 + _rot_half(x, LANE) * s128
        k_ref[:, sl] = r
        kb_ref[:, sl] = r.astype(BF16)
    v = dv_ref[...]
    v_ref[...] = v
    vb_ref[...] = v.astype(BF16)
    for c in range(IDX_HEADS * IDX_D // LANE):
        sl = slice(c * LANE, (c + 1) * LANE)
        x = iq_ref[:, sl]
        qi_ref[:, sl] = (x * c64 + _rot_half(x, IDX_D) * s64).astype(BF16)
    misc = misc_ref[...]
    lane = lax.broadcasted_iota(I32, misc.shape, 1)
    isk = lane < IDX_D
    mu = jnp.sum(jnp.where(isk, misc, 0.0), -1, keepdims=True) * (1.0 / IDX_D)
    xc = jnp.where(isk, misc - mu, 0.0)
    var = jnp.sum(xc * xc, -1, keepdims=True) * (1.0 / IDX_D)
    kn = xc * lax.rsqrt(var + LN_EPS) * ig_ref[...] + ib_ref[...]
    kr = kn * c64 + _rot_half(kn, IDX_D) * s64
    ki_ref[...] = kr[:, :IDX_D]
    kib_ref[...] = kr[:, :IDX_D].astype(BF16)
    w = pltpu.roll(misc, LANE - IDX_D, 1)
    wi_ref[...] = jnp.where(lane < IDX_HEADS, w * (IDX_HEADS ** -0.5) * (IDX_D ** -0.5), 0.0)


def dsa_prep(z, tabs, idx_g, idx_b):
    t = z.shape[0]
    tb = _tile(t, 512)
    c128, s128, c64, s64 = tabs
    zspec = lambda w, off: pl.BlockSpec((tb, w), lambda i, o=off // w: (i, o))
    row = lambda w: pl.BlockSpec((tb, w), lambda i: (i, 0))
    pad = lambda a: jnp.pad(a, (0, LANE - IDX_D)).reshape(1, LANE)
    return pl.pallas_call(
        _prep_kernel,
        out_shape=[jax.ShapeDtypeStruct((t, 1024), BF16),
                   jax.ShapeDtypeStruct((t, 512), F32),
                   jax.ShapeDtypeStruct((t, 512), BF16),
                   jax.ShapeDtypeStruct((t, 512), F32),
                   jax.ShapeDtypeStruct((t, 512), BF16),
                   jax.ShapeDtypeStruct((t, 512), BF16),
                   jax.ShapeDtypeStruct((t, IDX_D), F32),
                   jax.ShapeDtypeStruct((t, IDX_D), BF16),
                   jax.ShapeDtypeStruct((t, LANE), F32)],
        grid=(t // tb,),
        in_specs=[zspec(1024, Z_DQ), zspec(512, Z_DK), zspec(512, Z_DV), zspec(512, Z_IQ), zspec(LANE, Z_MISC),
                  row(LANE), row(LANE), row(LANE), row(LANE), _cst((1, LANE)), _cst((1, LANE))],
        out_specs=[row(1024), row(512), row(512), row(512), row(512), row(512), row(IDX_D), row(IDX_D), row(LANE)],
        compiler_params=_cparams(("parallel",)),
        name="dsa_prep",
    )(z, z, z, z, z, c128, s128, c64, s64, pad(idx_g), pad(idx_b))


def rope_tables(pos):
    posf = pos.astype(F32)[:, None]

    def tab(width):
        half = width // 2
        inv = ROPE_THETA ** (-jnp.arange(half, dtype=F32) / half)
        ang = posf * inv
        c, s = jnp.cos(ang), jnp.sin(ang)
        reps = LANE // width
        return jnp.tile(jnp.concatenate([c, c], 1), (1, reps)), jnp.tile(jnp.concatenate([-s, s], 1), (1, reps))

    c128, s128 = tab(DSA_HD)
    c64, s64 = tab(IDX_D)
    return c128, s128, c64, s64


def _log_sigmoid(x):
    return jnp.minimum(x, 0.0) - jnp.log1p(jnp.exp(-jnp.abs(x)))


def _gla_kernel(q_ref, k_ref, v_ref, r_ref, glr_ref, w2_ref, bg_ref, ng_ref, o_ref, st_ref, s_ref, *, tb):
    c = GLA_CHUNK
    j = pl.program_id(1)

    @pl.when(j == 0)
    def _():
        s_ref[...] = jnp.zeros_like(s_ref)

    ri = lax.broadcasted_iota(I32, (c, c), 0)
    ci = lax.broadcasted_iota(I32, (c, c), 1)
    causal = ci <= ri
    tri = causal.astype(F32)
    w2 = w2_ref[...]
    bg = bg_ref[...]
    ng = ng_ref[...]

    def chunk(ic, carry):
        r0 = pl.multiple_of(ic * c, c)
        q = q_ref[pl.ds(r0, c), :] * (GLA_DK ** -0.5)
        k = k_ref[pl.ds(r0, c), :]
        v = v_ref[pl.ds(r0, c), :].astype(BF16)
        x = _dot(glr_ref[pl.ds(r0, c), :].astype(BF16), w2) + bg
        g = _log_sigmoid(x) * (1.0 / GLA_TAU)
        b = jnp.dot(tri, g, preferred_element_type=F32, precision=lax.Precision.HIGHEST)
        bm = b[c // 2 - 1:c // 2, :]
        bl = b[c - 1:c, :]
        st = s_ref[...]
        qe = (q * jnp.exp(b - bm)).astype(BF16)
        ke = (k * jnp.exp(bm - b)).astype(BF16)
        a = jnp.where(causal, _dot_nt(qe, ke), 0.0)
        o = _dot_nt((q * jnp.exp(b)).astype(BF16), st.astype(BF16)) + _dot(a.astype(BF16), v)
        kd = (k * jnp.exp(bl - b)).astype(BF16)
        s_ref[...] = st * jnp.exp(bl) + _dot_tn(v, kd)
        on = o * lax.rsqrt(jnp.mean(o * o, -1, keepdims=True) + LN_EPS) * ng
        o_ref[pl.ds(r0, c), :] = on * _silu(r_ref[pl.ds(r0, c), :])
        return carry

    lax.fori_loop(0, tb // c, chunk, 0)

    @pl.when(j == pl.num_programs(1) - 1)
    def _():
        st_ref[0] = s_ref[...]


def gla_prompt(z, w2, bg, ng):
    t = z.shape[0]
    tb = _tile(t, 512)
    assert tb % GLA_CHUNK == 0
    h = GLA_HEADS
    w2p = jnp.pad(w2, ((0, LANE - GLA_RANK), (0, 0))).astype(BF16)
    o, st = pl.pallas_call(
        functools.partial(_gla_kernel, tb=tb),
        out_shape=[jax.ShapeDtypeStruct((t, h * GLA_DV), F32),
                   jax.ShapeDtypeStruct((h, GLA_DV, GLA_DK), F32)],
        grid=(h, t // tb),
        in_specs=[pl.BlockSpec((tb, GLA_DK), lambda hh, j: (j, Z_GQ // GLA_DK + hh)),
                  pl.BlockSpec((tb, GLA_DK), lambda hh, j: (j, Z_GK // GLA_DK + hh)),
                  pl.BlockSpec((tb, GLA_DV), lambda hh, j: (j, Z_GV // GLA_DV + hh)),
                  pl.BlockSpec((tb, GLA_DV), lambda hh, j: (j, Z_GR // GLA_DV + hh)),
                  pl.BlockSpec((tb, LANE), lambda hh, j: (j, Z_GLR // LANE)),
                  pl.BlockSpec((LANE, GLA_DK), lambda hh, j: (0, hh)),
                  pl.BlockSpec((1, GLA_DK), lambda hh, j: (0, hh)),
                  pl.BlockSpec((1, GLA_DV), lambda hh, j: (0, 0))],
        out_specs=[pl.BlockSpec((tb, GLA_DV), lambda hh, j: (j, hh)),
                   pl.BlockSpec((1, GLA_DV, GLA_DK), lambda hh, j: (hh, 0, 0))],
        scratch_shapes=[pltpu.VMEM((GLA_DV, GLA_DK), F32)],
        compiler_params=_cparams(("parallel", "arbitrary")),
        name="gla_prompt",
    )(z, z, z, z, z, w2p, bg.reshape(1, -1), ng.reshape(1, -1))
    return o, jnp.swapaxes(st, 1, 2)


def _sort_key(x):
    bits = lax.bitcast_convert_type(x + 0.0, I32)
    return bits ^ ((bits >> 31) & 0x7FFFFFFF)


def _kth_largest_key(count_ge, k_sel):
    kf = float(k_sel)
    c_adm = count_ge(INT_MIN + 1)
    c0 = count_ge(0)
    nonneg = c0 >= kf
    lo = jnp.where(nonneg, 0, INT_MIN + 1).astype(I32)
    cnt = jnp.where(nonneg, c0, c_adm)

    def cond(st):
        return (st[0] < 31) & st[3]

    def bit_step(st):
        it, lo, cnt, _ = st
        cand = lo + jnp.left_shift(jnp.int32(1), 30 - it)
        c = count_ge(cand)
        take = c >= kf
        cnt = jnp.where(take, c, cnt)
        return it + 1, jnp.where(take, cand, lo), cnt, jnp.max(cnt) > kf

    _, tau, n_ge, _ = lax.while_loop(cond, bit_step, (jnp.int32(0), lo, cnt, jnp.max(cnt) > kf))
    n_gt = count_ge(tau + 1)
    need = kf - n_gt
    tie = n_ge - kf
    return tau, need, tie


def _dsa_kernel(q_ref, qi_ref, wi_ref, kit_ref, kt_ref, v_ref, o_ref,
                key_ref, tri_ref, q2_ref, wb_ref, m_ref, l_ref, acc_ref, *, tq, tk, ts, k_sel):
    i = pl.program_id(0)

    @pl.when(i == 0)
    def _():
        r = lax.broadcasted_iota(I32, (tk, tk), 0)
        c = lax.broadcasted_iota(I32, (tk, tk), 1)
        tri_ref[...] = (r <= c).astype(BF16)

    t0 = i * tq
    n_kt = (t0 + tq + tk - 1) // tk
    wi = wi_ref[...]
    for n in range(DSA_KV_HEADS):
        for g in range(DSA_G):
            h = n * DSA_G + g
            q2_ref[n, g * tq:(g + 1) * tq, :] = q_ref[:, h * DSA_HD:(h + 1) * DSA_HD]

    row_s = t0 + lax.broadcasted_iota(I32, (tq, ts), 0)

    for h in range(IDX_HEADS):
        wb_ref[h] = jnp.broadcast_to(wi[:, h:h + 1], (tq, LANE))

    def score_chunk(kc, carry):
        c0 = pl.multiple_of(kc * ts, ts)
        acc = jnp.zeros((tq, ts), F32)
        for h in range(IDX_HEADS):
            sc = _dot(qi_ref[:, h * IDX_D:(h + 1) * IDX_D], kit_ref[:, pl.ds(c0, ts)])
            acc = acc + jnp.maximum(sc, 0.0) * jnp.concatenate([wb_ref[h]] * (ts // LANE), axis=1)
        col = c0 + lax.broadcasted_iota(I32, (tq, ts), 1)
        key_ref[:, pl.ds(c0, ts)] = jnp.where(col <= row_s, _sort_key(acc), INT_MIN)
        return carry

    lax.fori_loop(0, n_kt * (tk // ts), score_chunk, 0)

    nl = tk // LANE

    def count_ge(cand):
        cb = jnp.broadcast_to(cand, (tq, LANE))

        def body(kt, acc):
            c0 = pl.multiple_of(kt * tk, tk)
            for j in range(nl):
                acc = acc + jnp.where(key_ref[:, pl.ds(c0 + j * LANE, LANE)] >= cb, 1, 0)
            return acc

        acc = lax.fori_loop(0, n_kt, body, jnp.zeros((tq, LANE), I32))
        return jnp.sum(acc.astype(F32), axis=1, keepdims=True)

    tau, need, tie = _kth_largest_key(count_ge, k_sel)
    tau_eff = jnp.maximum(tau, INT_MIN + 1)
    has_tie = jnp.max(tie) > 0.0

    m_ref[...] = jnp.full_like(m_ref, NEG_BIG)
    l_ref[...] = jnp.zeros_like(l_ref)
    acc_ref[...] = jnp.zeros_like(acc_ref)

    def attend(kt, run_eq):
        c0 = pl.multiple_of(kt * tk, tk)
        keys = key_ref[:, pl.ds(c0, tk)]

        def fast(run_eq):
            return jnp.where(keys >= tau_eff, 0.0, NEG_BIG), run_eq

        def slow(run_eq):
            eq = (keys == tau) & (keys > INT_MIN)
            rank = run_eq + _dot(eq.astype(BF16), tri_ref[...])
            sel = (keys > tau) | (eq & (rank <= need))
            return jnp.where(sel, 0.0, NEG_BIG), run_eq + jnp.sum(eq.astype(F32), axis=1, keepdims=True)

        bias, run_eq = lax.cond(has_tie, slow, fast, run_eq)
        bias2 = jnp.concatenate([bias] * DSA_G, axis=0)
        for n in range(DSA_KV_HEADS):
            ktile = kt_ref[n * DSA_HD:(n + 1) * DSA_HD, pl.ds(c0, tk)]
            vtile = v_ref[pl.ds(c0, tk), n * DSA_HD:(n + 1) * DSA_HD]
            s = _dot(q2_ref[n], ktile) + bias2
            m_old = m_ref[n]
            m_new = jnp.maximum(m_old, jnp.max(s, axis=1, keepdims=True))
            p = jnp.exp2(s - m_new)
            alpha = jnp.exp2(m_old - m_new)
            l_ref[n] = alpha * l_ref[n] + jnp.sum(p, axis=1, keepdims=True)
            acc_ref[n] = alpha * acc_ref[n] + _dot(p.astype(BF16), vtile)
            m_ref[n] = m_new
        return run_eq

    lax.fori_loop(0, n_kt, attend, jnp.zeros((tq, 1), F32))
    for n in range(DSA_KV_HEADS):
        o = acc_ref[n] / l_ref[n]
        for g in range(DSA_G):
            h = n * DSA_G + g
            o_ref[:, h * DSA_HD:(h + 1) * DSA_HD] = o[g * tq:(g + 1) * tq, :]


def dsa_prompt(q_bf, qi_bf, wi, kit_bf, kt_bf, v_bf):
    t = q_bf.shape[0]
    tq = _tile(t, 128)
    tk = _tile(t, 1024)
    ts = _tile(tk, 256)
    k_sel = min(TOPK_MAX, t // 4)
    whole = lambda shape: pl.BlockSpec(shape, lambda i: (0, 0), pipeline_mode=pl.Buffered(1))
    return pl.pallas_call(
        functools.partial(_dsa_kernel, tq=tq, tk=tk, ts=ts, k_sel=k_sel),
        out_shape=jax.ShapeDtypeStruct((t, DSA_HEADS * DSA_HD), F32),
        grid=(t // tq,),
        in_specs=[pl.BlockSpec((tq, DSA_HEADS * DSA_HD), lambda i: (i, 0)),
                  pl.BlockSpec((tq, IDX_HEADS * IDX_D), lambda i: (i, 0)),
                  pl.BlockSpec((tq, LANE), lambda i: (i, 0)),
                  whole((IDX_D, t)), whole((DSA_KV_HEADS * DSA_HD, t)), whole((t, DSA_KV_HEADS * DSA_HD))],
        out_specs=pl.BlockSpec((tq, DSA_HEADS * DSA_HD), lambda i: (i, 0)),
        scratch_shapes=[pltpu.VMEM((tq, t), I32),
                        pltpu.VMEM((tk, tk), BF16),
                        pltpu.VMEM((DSA_KV_HEADS, DSA_G * tq, DSA_HD), BF16),
                        pltpu.VMEM((IDX_HEADS, tq, LANE), F32),
                        pltpu.VMEM((DSA_KV_HEADS, DSA_G * tq, 1), F32),
                        pltpu.VMEM((DSA_KV_HEADS, DSA_G * tq, 1), F32),
                        pltpu.VMEM((DSA_KV_HEADS, DSA_G * tq, DSA_HD), F32)],
        compiler_params=_cparams(("arbitrary",)),
        name="dsa_prompt",
    )(q_bf, qi_bf, wi, kit_bf, kt_bf, v_bf)


def _sdsa_select_kernel(pt_ref, qi_ref, w_ref, kn_ref, *rest, ppc, ts, k_sel, past):
    pages = rest[:ppc]
    keys_out, tau_out, need_out, tie_out = rest[ppc:ppc + 4]
    key_ref, kc_ref = rest[ppc + 4:]
    c = pl.program_id(1)
    nch = pl.num_programs(1)
    qi = qi_ref[0]
    w = w_ref[0]

    def token_scores(sc):
        x = jnp.maximum(sc, 0.0) * w
        parts = [jnp.sum(x[t * IDX_HEADS:(t + 1) * IDX_HEADS], axis=0, keepdims=True) for t in range(ts)]
        return jnp.concatenate(parts + [jnp.zeros((SUB - ts, sc.shape[1]), F32)], axis=0)

    for j in range(ppc):
        kc_ref[j * PAGE_SIZE:(j + 1) * PAGE_SIZE, :] = pages[j][0, 0].astype(BF16)
    width = ppc * PAGE_SIZE
    score = token_scores(_dot_nt(qi, kc_ref[...]))
    rows = lax.broadcasted_iota(I32, (SUB, width), 0)
    key_ref[:, pl.ds(pl.multiple_of(c * width, width), width)] = jnp.where(rows < ts, _sort_key(score), INT_MIN)

    @pl.when(c == nch - 1)
    def _():
        kn = jnp.concatenate([kn_ref[0], jnp.zeros((LANE - ts, IDX_D), F32)], axis=0).astype(BF16)
        sc = token_scores(_dot_nt(qi, kn))
        r8 = lax.broadcasted_iota(I32, (SUB, LANE), 0)
        l8 = lax.broadcasted_iota(I32, (SUB, LANE), 1)
        ok = (r8 < ts) & (l8 <= r8)
        key_ref[:, past:past + LANE] = jnp.where(ok, _sort_key(sc), INT_MIN)

        def count_ge(cand):
            return jnp.sum(jnp.where(key_ref[...] >= cand, 1.0, 0.0), axis=1, keepdims=True)

        tau, need, tie = _kth_largest_key(count_ge, k_sel)
        keys_out[0] = key_ref[...]
        tau_out[0] = jnp.broadcast_to(tau, (SUB, LANE))
        need_out[0] = jnp.broadcast_to(need, (SUB, LANE))
        tie_out[0] = jnp.broadcast_to(tie, (SUB, LANE))


def _sdsa_attend_kernel(pt_ref, q_ref, keys_ref, ktail_ref, tau_ref, need_ref, tie_ref, kn_ref, vn_ref, *rest,
                        ppc, ts):
    kpages = rest[:ppc]
    vpages = rest[ppc:2 * ppc]
    o_ref = rest[2 * ppc]
    m_ref, l_ref, acc_ref, run_ref, tri_ref, ex_ref, kc_ref, vc_ref = rest[2 * ppc + 1:]
    c = pl.program_id(1)
    nch = pl.num_programs(1)
    width = ppc * PAGE_SIZE
    nq = DSA_KV_HEADS * SQ_ROWS
    prow = PAGE_SIZE * DSA_KV_HEADS

    @pl.when(c == 0)
    def _():
        m_ref[...] = jnp.full_like(m_ref, NEG_BIG)
        l_ref[...] = jnp.zeros_like(l_ref)
        acc_ref[...] = jnp.zeros_like(acc_ref)
        run_ref[...] = jnp.zeros_like(run_ref)
        r = lax.broadcasted_iota(I32, (width, width), 0)
        cc = lax.broadcasted_iota(I32, (width, width), 1)
        tri_ref[...] = (r <= cc).astype(BF16)
        er = lax.broadcasted_iota(I32, (PAGE_SIZE, prow), 0)
        ec = lax.broadcasted_iota(I32, (PAGE_SIZE, prow), 1)
        ex_ref[...] = ((ec >= er * DSA_KV_HEADS) & (ec < (er + 1) * DSA_KV_HEADS)).astype(BF16)

    tau = tau_ref[0][:, 0:1]
    need = need_ref[0][:, 0:1]
    tau_eff = jnp.maximum(tau, INT_MIN + 1)
    has_tie = jnp.max(tie_ref[0]) > 0.0
    q = q_ref[0]

    def process(keys, kb, vb):
        w = keys.shape[1]
        wi_ = w * DSA_KV_HEADS

        def fast(run_eq):
            return jnp.where(keys >= tau_eff, 1.0, 0.0), run_eq

        def slow(run_eq):
            eq = (keys == tau) & (keys > INT_MIN)
            rank = run_eq + _dot(eq.astype(BF16), tri_ref[0:w, 0:w])
            sel = (keys > tau) | (eq & (rank <= need))
            return jnp.where(sel, 1.0, 0.0), run_eq + jnp.sum(eq.astype(F32), axis=1, keepdims=True)

        sel01, run_eq = lax.cond(has_tie, slow, fast, run_ref[...])
        run_ref[...] = run_eq
        selb = sel01.astype(BF16)
        sel_x = jnp.concatenate([_dot(selb[:, j * PAGE_SIZE:(j + 1) * PAGE_SIZE], ex_ref[...])
                                 for j in range(w // PAGE_SIZE)], axis=1)
        bias_t = (sel_x - 1.0) * (-NEG_BIG)
        col = lax.broadcasted_iota(I32, (SQ_ROWS, wi_), 1)
        bias_t2 = jnp.concatenate([bias_t] * DSA_G, axis=0)
        bias_all = jnp.concatenate(
            [bias_t2 + jnp.where(col % DSA_KV_HEADS == n, 0.0, NEG_BIG) for n in range(DSA_KV_HEADS)], axis=0)
        s = _dot_nt(q, kb) + bias_all
        m_old = m_ref[...]
        m_new = jnp.maximum(m_old, jnp.max(s, axis=1, keepdims=True))
        p = jnp.exp2(s - m_new)
        alpha = jnp.exp2(m_old - m_new)
        l_ref[...] = alpha * l_ref[...] + jnp.sum(p, axis=1, keepdims=True)
        acc_ref[...] = alpha * acc_ref[...] + _dot(p.astype(BF16), vb)
        m_ref[...] = m_new

    for j in range(ppc):
        kc_ref[j * prow:(j + 1) * prow, :] = kpages[j][0, 0].astype(BF16)
        vc_ref[j * prow:(j + 1) * prow, :] = vpages[j][0, 0].astype(BF16)
    process(keys_ref[0], kc_ref[...], vc_ref[...])

    @pl.when(c == nch - 1)
    def _():
        pad = jnp.zeros((prow - ts * DSA_KV_HEADS, DSA_HD), F32)
        kn = jnp.concatenate([kn_ref[0], pad], axis=0).astype(BF16)
        vn = jnp.concatenate([vn_ref[0], pad], axis=0).astype(BF16)
        process(ktail_ref[0], kn, vn)
        o_ref[0] = acc_ref[...] / l_ref[...]


def dsa_sample(q_bf, qi_bf, wi, ki_new, kd_new, vd_new, cache_k, cache_v, cache_idx_k, page_table, layer, bs, ts):
    assert ts <= SUB
    n_pages = page_table.shape[1]
    past = n_pages * PAGE_SIZE
    ppc = 8 if n_pages % 8 == 0 else n_pages
    nch = n_pages // ppc
    width = ppc * PAGE_SIZE
    nk = past + LANE
    k_sel = min(TOPK_MAX, (past + ts) // 4)
    dkv = DSA_KV_HEADS * DSA_HD
    pt = page_table.reshape(-1).astype(I32)
    qi_r = qi_bf.reshape(bs, ts * IDX_HEADS, IDX_D)
    w_r = wi[:, :IDX_HEADS].reshape(bs, ts * IDX_HEADS, 1)
    cik = cache_idx_k

    def page_spec(tail, j, rows=PAGE_SIZE):
        return pl.BlockSpec((1, 1, rows) + tail,
                            lambda b, c, ptr, j=j: (layer, ptr[b * n_pages + c * ppc + j]) + (0,) * (1 + len(tail)))

    per_b = lambda shape: pl.BlockSpec((1,) + shape, lambda b, c, ptr: (b,) + tuple(0 for _ in shape))
    sel_spec = pltpu.PrefetchScalarGridSpec(
        num_scalar_prefetch=1, grid=(bs, nch),
        in_specs=[per_b((ts * IDX_HEADS, IDX_D)), per_b((ts * IDX_HEADS, 1)), per_b((ts, IDX_D))]
                 + [page_spec((IDX_D,), j) for j in range(ppc)],
        out_specs=[per_b((SUB, nk)), per_b((SUB, LANE)), per_b((SUB, LANE)), per_b((SUB, LANE))],
        scratch_shapes=[pltpu.VMEM((SUB, nk), I32), pltpu.VMEM((width, IDX_D), BF16)])
    keys, tau, need, tie = pl.pallas_call(
        functools.partial(_sdsa_select_kernel, ppc=ppc, ts=ts, k_sel=k_sel, past=past),
        out_shape=[jax.ShapeDtypeStruct((bs, SUB, nk), I32), jax.ShapeDtypeStruct((bs, SUB, LANE), I32),
                   jax.ShapeDtypeStruct((bs, SUB, LANE), F32), jax.ShapeDtypeStruct((bs, SUB, LANE), F32)],
        grid_spec=sel_spec,
        compiler_params=_cparams(("parallel", "arbitrary")),
        name="dsa_sample_select",
    )(pt, qi_r, w_r, ki_new.reshape(bs, ts, IDX_D), *([cik] * ppc))

    qx = q_bf.reshape(bs, ts, DSA_KV_HEADS, DSA_G, DSA_HD).transpose(0, 2, 3, 1, 4)
    nq = DSA_KV_HEADS * SQ_ROWS
    qx = jnp.pad(qx, ((0, 0), (0, 0), (0, 0), (0, SUB - ts), (0, 0))).reshape(bs, nq, DSA_HD)
    prow = PAGE_SIZE * DSA_KV_HEADS
    rows_view = lambda a: a.reshape(a.shape[0], a.shape[1], prow, DSA_HD)
    att_spec = pltpu.PrefetchScalarGridSpec(
        num_scalar_prefetch=1, grid=(bs, nch),
        in_specs=[per_b((nq, DSA_HD)),
                  pl.BlockSpec((1, SUB, width), lambda b, c, ptr: (b, 0, c)),
                  pl.BlockSpec((1, SUB, LANE), lambda b, c, ptr: (b, 0, past // LANE)),
                  per_b((SUB, LANE)), per_b((SUB, LANE)), per_b((SUB, LANE)),
                  per_b((ts * DSA_KV_HEADS, DSA_HD)), per_b((ts * DSA_KV_HEADS, DSA_HD))]
                 + [page_spec((DSA_HD,), j % ppc, prow) for j in range(2 * ppc)],
        out_specs=per_b((nq, DSA_HD)),
        scratch_shapes=[pltpu.VMEM((nq, 1), F32), pltpu.VMEM((nq, 1), F32), pltpu.VMEM((nq, DSA_HD), F32),
                        pltpu.VMEM((SUB, 1), F32), pltpu.VMEM((width, width), BF16),
                        pltpu.VMEM((PAGE_SIZE, prow), BF16),
                        pltpu.VMEM((ppc * prow, DSA_HD), BF16), pltpu.VMEM((ppc * prow, DSA_HD), BF16)])
    o = pl.pallas_call(
        functools.partial(_sdsa_attend_kernel, ppc=ppc, ts=ts),
        out_shape=jax.ShapeDtypeStruct((bs, nq, DSA_HD), F32),
        grid_spec=att_spec,
        compiler_params=_cparams(("parallel", "arbitrary")),
        name="dsa_sample_attend",
    )(pt, qx, keys, keys, tau, need, tie,
      kd_new.reshape(bs, ts * DSA_KV_HEADS, DSA_HD), vd_new.reshape(bs, ts * DSA_KV_HEADS, DSA_HD),
      *([rows_view(cache_k)] * ppc), *([rows_view(cache_v)] * ppc))
    o = o.reshape(bs, DSA_KV_HEADS, DSA_G, SUB, DSA_HD)[:, :, :, :ts]
    return o.transpose(0, 3, 1, 2, 4).reshape(bs * ts, DSA_HEADS * DSA_HD)


def _gelu_tanh(x):
    return 0.5 * x * (1.0 + jnp.tanh(math.sqrt(2.0 / math.pi) * (x + 0.044715 * x * x * x)))


def _softplus(x):
    return jnp.maximum(x, 0.0) + jnp.log1p(jnp.exp(-jnp.abs(x)))


def _lru_gates(xc, wa, ba, wx, bx, lam):
    xcb = xc.astype(BF16)
    gate_r = _sigmoid(_dot(xcb, wa) + ba)
    gate_i = _sigmoid(_dot(xcb, wx) + bx)
    log_a = -LRU_C * gate_r * _softplus(-lam)
    th = jnp.tanh(log_a)
    mult = jnp.sqrt(-2.0 * th / (1.0 - th))
    return jnp.exp(log_a), mult, gate_i * xc


def _lru_kernel(x_ref, y_ref, cw_ref, cb_ref, wa_ref, ba_ref, wx_ref, bx_ref, lam_ref,
                o_ref, hl_ref, xp_ref, h_ref, *, tb):
    i = pl.program_id(0)

    @pl.when(i == 0)
    def _():
        xp_ref[0:8, :] = jnp.zeros((8, LRU_W), F32)
        h_ref[...] = jnp.zeros_like(h_ref)

    xp_ref[8:8 + tb, :] = x_ref[...]
    rows = lax.broadcasted_iota(I32, (tb, LRU_BW), 0)
    first = (rows == 0) & (i == 0)
    for n in range(LRU_BLOCKS):
        sl = slice(n * LRU_BW, (n + 1) * LRU_BW)
        xc = jnp.broadcast_to(cb_ref[:, sl], (tb, LRU_BW))
        for w in range(CONV_W):
            xc = xc + xp_ref[8 - (CONV_W - 1) + w:8 - (CONV_W - 1) + w + tb, sl] * cw_ref[w:w + 1, sl]
        a, mult, gx = _lru_gates(xc, wa_ref[n], ba_ref[:, sl], wx_ref[n], bx_ref[:, sl], lam_ref[:, sl])
        b = jnp.where(first, 1.0, mult) * gx
        d = 1
        while d < tb:
            a_sh = pltpu.roll(a, d, 0)
            b_sh = pltpu.roll(b, d, 0)
            ok = rows >= d
            b = jnp.where(ok, a * b_sh + b, b)
            a = jnp.where(ok, a * a_sh, a)
            d *= 2
        h = a * h_ref[:, sl] + b
        h_ref[:, sl] = h[tb - 1:tb, :]
        o_ref[:, sl] = h * _gelu_tanh(y_ref[:, sl])
    xp_ref[0:8, :] = xp_ref[tb:tb + 8, :]
    hl_ref[...] = h_ref[...]


def rglru_prompt(z, conv_w, conv_b, wa, ba, wx, bx, lam):
    t = z.shape[0]
    tb = _tile(t, 256)
    vec = lambda a: a.reshape(1, LRU_W)
    return pl.pallas_call(
        functools.partial(_lru_kernel, tb=tb),
        out_shape=[jax.ShapeDtypeStruct((t, LRU_W), F32), jax.ShapeDtypeStruct((1, LRU_W), F32)],
        grid=(t // tb,),
        in_specs=[pl.BlockSpec((tb, LRU_W), lambda i: (i, Z_LX // LRU_W)),
                  pl.BlockSpec((tb, LRU_W), lambda i: (i, Z_LY // LRU_W)),
                  _cst((CONV_W, LRU_W)), _cst((1, LRU_W)),
                  _cst((LRU_BLOCKS, LRU_BW, LRU_BW)), _cst((1, LRU_W)),
                  _cst((LRU_BLOCKS, LRU_BW, LRU_BW)), _cst((1, LRU_W)), _cst((1, LRU_W))],
        out_specs=[pl.BlockSpec((tb, LRU_W), lambda i: (i, 0)), _cst((1, LRU_W))],
        scratch_shapes=[pltpu.VMEM((tb + 8, LRU_W), F32), pltpu.VMEM((1, LRU_W), F32)],
        compiler_params=_cparams(("arbitrary",)),
        name="rglru_prompt",
    )(z, z, conv_w, vec(conv_b), wa.astype(BF16), vec(ba), wx.astype(BF16), vec(bx), vec(lam))


def _lru_sample_kernel(xs_ref, y_ref, h0_ref, cw_ref, cb_ref, wa_ref, ba_ref, wx_ref, bx_ref, lam_ref,
                       o_ref, hl_ref, *, ts, first_is_start):
    h = h0_ref[0]
    for t in range(ts):
        xc = cb_ref[...]
        for w in range(CONV_W):
            xc = xc + xs_ref[t + w] * cw_ref[w:w + 1, :]
        a, mult, gx = _lru_gates(xc, wa_ref[0], ba_ref[...], wx_ref[0], bx_ref[...], lam_ref[...])
        if first_is_start and t == 0:
            mult = jnp.ones_like(mult)
        h = a * h + mult * gx
        o_ref[t] = h * _gelu_tanh(y_ref[t])
    hl_ref[...] = h


def rglru_sample(xs, ly, state_lru, layer, conv_w, conv_b, wa, ba, wx, bx, lam, first_is_start):
    ts, b, w = ly.shape
    vec = lambda a: a.reshape(1, w)
    blk = lambda r: pl.BlockSpec((r, b, LRU_BW), lambda n: (0, 0, n))
    col = lambda r: pl.BlockSpec((r, LRU_BW), lambda n: (0, n))
    wsp = pl.BlockSpec((1, LRU_BW, LRU_BW), lambda n: (n, 0, 0))
    return pl.pallas_call(
        functools.partial(_lru_sample_kernel, ts=ts, first_is_start=first_is_start),
        out_shape=[jax.ShapeDtypeStruct((ts, b, w), F32), jax.ShapeDtypeStruct((b, w), F32)],
        grid=(LRU_BLOCKS,),
        in_specs=[blk(CONV_W - 1 + ts), blk(ts),
                  pl.BlockSpec((1, b, LRU_BW), lambda n: (layer, 0, n)),
                  col(CONV_W), col(1), wsp, col(1), wsp, col(1), col(1)],
        out_specs=[blk(ts), pl.BlockSpec((b, LRU_BW), lambda n: (0, n))],
        compiler_params=_cparams(("parallel",)),
        name="rglru_sample",
    )(xs, ly, state_lru, conv_w, vec(conv_b), wa.astype(BF16), vec(ba), wx.astype(BF16), vec(bx), vec(lam))


def _gla_gate_kernel(glr_ref, w2_ref, bg_ref, o_ref):
    x = _dot(glr_ref[...].astype(BF16), w2_ref[...]) + bg_ref[...]
    o_ref[...] = _log_sigmoid(x) * (1.0 / GLA_TAU)


def gla_gate(z, w2, bg):
    t = z.shape[0]
    w = GLA_HEADS * GLA_DK
    w2p = jnp.pad(w2, ((0, LANE - GLA_RANK), (0, 0))).astype(BF16)
    return pl.pallas_call(
        _gla_gate_kernel,
        out_shape=jax.ShapeDtypeStruct((t, w), F32),
        grid=(1,),
        in_specs=[pl.BlockSpec((t, LANE), lambda i: (0, Z_GLR // LANE)), _cst((LANE, w)), _cst((1, w))],
        out_specs=_cst((t, w)),
        compiler_params=_cparams(("arbitrary",)),
        name="gla_gate",
    )(z, w2p, bg.reshape(1, w))


def _gla_sample_kernel(qt_ref, kt_ref, gt_ref, v_ref, r_ref, s0_ref, ng_ref, o_ref, s_ref, *, ts):
    ng = ng_ref[...]
    for h in range(GLA_HEADS):
        s = s0_ref[0, 0, h]
        qt = qt_ref[0, h] * (GLA_DK ** -0.5)
        kt = kt_ref[0, h]
        dec = jnp.exp(gt_ref[0, h])
        sl = slice(h * GLA_DV, (h + 1) * GLA_DV)
        for t in range(ts):
            s = dec[:, t:t + 1] * s + kt[:, t:t + 1] * v_ref[0, t:t + 1, sl]
            o = jnp.sum(qt[:, t:t + 1] * s, axis=0, keepdims=True)
            on = o * lax.rsqrt(jnp.mean(o * o, -1, keepdims=True) + LN_EPS) * ng
            o_ref[0, t:t + 1, sl] = on * _silu(r_ref[0, t:t + 1, sl])
        s_ref[0, h] = s


def gla_sample(qt, kt, gt, v, r, state_gla, layer, ng):
    b, h, dk, ts = qt.shape
    col = pl.BlockSpec((1, h, dk, ts), lambda i: (i, 0, 0, 0))
    row = pl.BlockSpec((1, ts, h * GLA_DV), lambda i: (i, 0, 0))
    return pl.pallas_call(
        functools.partial(_gla_sample_kernel, ts=ts),
        out_shape=[jax.ShapeDtypeStruct((b, ts, h * GLA_DV), F32),
                   jax.ShapeDtypeStruct((b, h, dk, GLA_DV), F32)],
        grid=(b,),
        in_specs=[col, col, col, row, row,
                  pl.BlockSpec((1, 1, h, dk, GLA_DV), lambda i: (layer, i, 0, 0, 0)), _cst((1, GLA_DV))],
        out_specs=[row, pl.BlockSpec((1, h, dk, GLA_DV), lambda i: (i, 0, 0, 0))],
        compiler_params=_cparams(("parallel",)),
        name="gla_sample",
    )(qt, kt, gt, v, r, state_gla, ng.reshape(1, GLA_DV))


def _mem_heads(q_of, k_of, v_of, store):
    scale = MEM_HD ** -0.5
    for h in range(MEM_HEADS):
        sl = slice(h * MEM_HD, (h + 1) * MEM_HD)
        s = _dot_nt(q_of(sl).astype(BF16), k_of(sl).astype(BF16)) * scale
        p = jnp.exp(s - jnp.max(s, -1, keepdims=True))
        p = p / jnp.sum(p, -1, keepdims=True)
        store(sl, _dot(p.astype(BF16), v_of(sl).astype(BF16)))


def _mem_kernel(q_ref, mk_ref, mv_ref, o_ref):
    def store(sl, val):
        o_ref[:, sl] = val
    _mem_heads(lambda sl: q_ref[:, sl], lambda sl: mk_ref[:, sl], lambda sl: mv_ref[:, sl], store)


def mem_attend_prompt(z, mkv):
    t = z.shape[0]
    tb = _tile(t, 512)
    w = MEM_HEADS * MEM_HD
    return pl.pallas_call(
        _mem_kernel,
        out_shape=jax.ShapeDtypeStruct((t, w), F32),
        grid=(t // tb,),
        in_specs=[pl.BlockSpec((tb, w), lambda i: (i, Z_MQ // w)),
                  pl.BlockSpec((N_MEM, w), lambda i: (0, 0)),
                  pl.BlockSpec((N_MEM, w), lambda i: (0, 1))],
        out_specs=pl.BlockSpec((tb, w), lambda i: (i, 0)),
        compiler_params=_cparams(("parallel",)),
        name="mem_attend",
    )(z, mkv, mkv)


def _mem_sample_kernel(q_ref, mk_ref, mv_ref, o_ref):
    def store(sl, val):
        o_ref[0, :, sl] = val
    head = lambda sl: sl.start // MEM_HD
    _mem_heads(lambda sl: q_ref[0, :, sl], lambda sl: mk_ref[0, 0, :, head(sl), :],
               lambda sl: mv_ref[0, 0, :, head(sl), :], store)


def mem_attend_sample(q, mem_k, mem_v, layer):
    b, rows, w = q.shape
    qs = pl.BlockSpec((1, rows, w), lambda i: (i, 0, 0))
    ms = pl.BlockSpec((1, 1, N_MEM, MEM_HEADS, MEM_HD), lambda i: (layer, i, 0, 0, 0))
    return pl.pallas_call(
        _mem_sample_kernel,
        out_shape=jax.ShapeDtypeStruct((b, rows, w), F32),
        grid=(b,),
        in_specs=[qs, ms, ms],
        out_specs=qs,
        compiler_params=_cparams(("parallel",)),
        name="mem_attend_sample",
    )(q, mem_k, mem_v)


def _merge_kernel(b0_ref, b1_ref, b2_ref, b3_ref, g0_ref, g1_ref, g2_ref, g3_ref, x_ref,
                  wb_ref, bg_ref, wo_ref, lg_ref, lb_ref, o_ref, *, alpha):
    brs = (b0_ref, b1_ref, b2_ref, b3_ref)
    gls = (g0_ref, g1_ref, g2_ref, g3_ref)
    merged = None
    for j in range(N_BRANCH):
        proj = _dot(brs[j][...].astype(BF16), wb_ref[j])
        term = _sigmoid(gls[j][...] + bg_ref[j:j + 1, :]) * proj
        merged = term if merged is None else merged + term
    y = _dot(merged.astype(BF16), wo_ref[...])
    o_ref[...] = _ln_rows(alpha * x_ref[...] + y, lg_ref[...], lb_ref[...])


def merge_out(branches, z, x, wb, bgate, wo, lg, lb, alpha):
    t = x.shape[0]
    tb = _tile(t, 256)
    d = D_MODEL
    row = pl.BlockSpec((tb, d), lambda i: (i, 0))
    gl = [pl.BlockSpec((tb, d), lambda i, o=Z_GL // d + j: (i, o)) for j in range(N_BRANCH)]
    return pl.pallas_call(
        functools.partial(_merge_kernel, alpha=alpha),
        out_shape=jax.ShapeDtypeStruct((t, d), F32),
        grid=(t // tb,),
        in_specs=[row, row, row, row] + gl + [row, _cst((N_BRANCH, d, d)), _cst((N_BRANCH, d)), _cst((d, d)),
                                               _cst((1, d)), _cst((1, d))],
        out_specs=row,
        compiler_params=_cparams(("parallel",)),
        name="merge_out",
    )(*branches, z, z, z, z, x, wb, bgate, wo, lg.reshape(1, d), lb.reshape(1, d))


def _first_argmax(x, n):
    m = jnp.max(x, axis=0, keepdims=True)
    ri = lax.broadcasted_iota(I32, x.shape, 0)
    idx = jnp.min(jnp.where(x == m, ri, n), axis=0, keepdims=True)
    return m, idx, ri


def _router_kernel(x_ref, rwt_ref, rb_ref, e_ref, w_ref, r_ref, c_ref, cnt_ref, tri_ref):
    i = pl.program_id(0)

    @pl.when(i == 0)
    def _():
        cnt_ref[...] = jnp.zeros_like(cnt_ref)
        r = lax.broadcasted_iota(I32, tri_ref.shape, 0)
        c = lax.broadcasted_iota(I32, tri_ref.shape, 1)
        tri_ref[...] = (r < c).astype(BF16)

    logits = lax.dot_general(rwt_ref[...], x_ref[...], (((1,), (1,)), ((), ())),
                             preferred_element_type=F32, precision=lax.Precision.HIGHEST)
    s = _sigmoid(logits)
    sb = s + rb_ref[...]
    gsz = N_EXPERTS // N_GROUPS
    neg = -jnp.inf
    gs = []
    for g in range(N_GROUPS):
        blk = sb[g * gsz:(g + 1) * gsz, :]
        m1, i1, ri = _first_argmax(blk, gsz)
        m2 = jnp.max(jnp.where(ri == i1, neg, blk), axis=0, keepdims=True)
        gs.append(m1 + m2)
    gscore = jnp.concatenate(gs, axis=0)
    gmask = jnp.zeros(gscore.shape, jnp.bool_)
    for _ in range(TOPK_GROUPS):
        _, gi, ri = _first_argmax(gscore, N_GROUPS)
        hit = ri == gi
        gmask = gmask | hit
        gscore = jnp.where(hit, neg, gscore)
    cand = jnp.concatenate(
        [jnp.where(gmask[g:g + 1, :], sb[g * gsz:(g + 1) * gsz, :], neg) for g in range(N_GROUPS)], axis=0)
    idxs, ws, ranks = [], [], []
    run = cnt_ref[...]
    for _ in range(TOP_K):
        _, ei, ri = _first_argmax(cand, N_EXPERTS)
        hit = ri == ei
        idxs.append(ei)
        ws.append(jnp.sum(jnp.where(hit, s, 0.0), axis=0, keepdims=True))
        cand = jnp.where(hit, neg, cand)
        hb = hit.astype(BF16)
        before = run + _dot(hb, tri_ref[...])
        ranks.append(jnp.sum(jnp.where(hit, before, 0.0), axis=0, keepdims=True))
        run = run + jnp.sum(hb.astype(F32), axis=1, keepdims=True)
    cnt_ref[...] = run
    c_ref[...] = run
    w = jnp.concatenate(ws, axis=0)
    e_ref[...] = jnp.concatenate(idxs, axis=0)
    w_ref[...] = w / jnp.sum(w, axis=0, keepdims=True) * ROUTED_SCALE
    r_ref[...] = jnp.concatenate(ranks, axis=0).astype(I32)


def moe_router(x, router_w, router_b):
    t, d = x.shape
    tb = _tile(t, 512)
    if tb % LANE != 0:
        tb = t
    tok = pl.BlockSpec((TOP_K, tb), lambda i: (0, i))
    return pl.pallas_call(
        _router_kernel,
        out_shape=[jax.ShapeDtypeStruct((TOP_K, t), I32), jax.ShapeDtypeStruct((TOP_K, t), F32),
                   jax.ShapeDtypeStruct((TOP_K, t), I32), jax.ShapeDtypeStruct((N_EXPERTS, 1), F32)],
        grid=(t // tb,),
        in_specs=[pl.BlockSpec((tb, d), lambda i: (i, 0)), _cst((N_EXPERTS, d)), _cst((N_EXPERTS, 1))],
        out_specs=[tok, tok, tok, _cst((N_EXPERTS, 1))],
        scratch_shapes=[pltpu.VMEM((N_EXPERTS, 1), F32), pltpu.VMEM((tb, tb), BF16)],
        compiler_params=_cparams(("arbitrary",)),
        name="moe_router",
    )(x, router_w.T, router_b.reshape(N_EXPERTS, 1))


def _dest_kernel(e_ref, r_ref, ps_ref, d_ref):
    ps = ps_ref[...]
    rows = []
    for k in range(TOP_K):
        e = e_ref[k:k + 1, :]
        ri = lax.broadcasted_iota(I32, (N_EXPERTS, e.shape[1]), 0)
        rows.append(jnp.sum(jnp.where(ri == e, ps, 0), axis=0, keepdims=True))
    d_ref[...] = jnp.concatenate(rows, axis=0) + r_ref[...]


def moe_dest(eidx, rank, pstarts):
    k, t = eidx.shape
    tb = _tile(t, 512)
    if tb % LANE != 0:
        tb = t
    tok = pl.BlockSpec((k, tb), lambda i: (0, i))
    return pl.pallas_call(
        _dest_kernel,
        out_shape=jax.ShapeDtypeStruct((k, t), I32),
        grid=(t // tb,),
        in_specs=[tok, tok, _cst((N_EXPERTS, 1))],
        out_specs=tok,
        compiler_params=_cparams(("parallel",)),
        name="moe_dest",
    )(eidx, rank, pstarts.reshape(N_EXPERTS, 1).astype(I32))


def _dispatch_kernel(dest_ref, x_ref, xp_in, xp_out, sem, *, tb):
    del xp_in

    def issue(r, c):
        for k in range(TOP_K):
            pltpu.make_async_copy(x_ref.at[r], xp_out.at[dest_ref[r * TOP_K + k]], sem).start()
        return c

    lax.fori_loop(0, tb, issue, 0)
    for k in range(TOP_K):
        pltpu.make_async_copy(x_ref, xp_out.at[pl.ds(0, tb)], sem).wait()


def _token_tile(t):
    tb = LANE
    while t % tb:
        tb //= 2
    assert tb * TOP_K >= LANE, "token count must be a multiple of 16"
    return tb


def moe_dispatch(x3, dest_flat, p):
    t = x3.shape[0]
    tb = _token_tile(t)
    return pl.pallas_call(
        functools.partial(_dispatch_kernel, tb=tb),
        out_shape=jax.ShapeDtypeStruct((p, SUB, LANE), F32),
        grid=(t // tb,),
        in_specs=[pl.BlockSpec((tb * TOP_K,), lambda i: (i,), memory_space=pltpu.SMEM),
                  pl.BlockSpec((tb, SUB, LANE), lambda i: (i, 0, 0)),
                  pl.BlockSpec(memory_space=pl.ANY)],
        out_specs=pl.BlockSpec(memory_space=pl.ANY),
        scratch_shapes=[pltpu.SemaphoreType.DMA(())],
        input_output_aliases={2: 0},
        compiler_params=_cparams(("arbitrary",)),
        name="moe_dispatch",
    )(dest_flat, x3, jnp.zeros((p, SUB, LANE), F32))


def _experts_kernel(be_ref, nv_ref, x_ref, w1_ref, w3_ref, w2_ref, o_ref, w1b, w3b, w2b, xs_ref):
    i = pl.program_id(0)
    e = be_ref[i]
    prev = be_ref[jnp.maximum(i - 1, 0)]

    @pl.when((i == 0) | (e != prev))
    def _():
        w1b[...] = w1_ref[0, 0].astype(BF16)
        w3b[...] = w3_ref[0, 0].astype(BF16)
        w2b[...] = w2_ref[0, 0].astype(BF16)

    @pl.when(i < nv_ref[0])
    def _():
        for s in range(SUB):
            xs_ref[:, s * LANE:(s + 1) * LANE] = x_ref[:, s, :].astype(BF16)
        x = xs_ref[...]
        h = _silu(_dot(x, w1b[...])) * _dot(x, w3b[...])
        y = _dot(h.astype(BF16), w2b[...])
        for s in range(SUB):
            o_ref[:, s, :] = y[:, s * LANE:(s + 1) * LANE]

    @pl.when(i >= nv_ref[0])
    def _():
        o_ref[...] = jnp.zeros_like(o_ref)


def moe_experts(xp, block_e, n_valid, w1, w3, w2, layer):
    p = xp.shape[0]
    d, de = w1.shape[2], w1.shape[3]
    nb = p // MOE_BLK
    tile = pl.BlockSpec((MOE_BLK, SUB, LANE), lambda i, be, nv: (i, 0, 0))
    grid_spec = pltpu.PrefetchScalarGridSpec(
        num_scalar_prefetch=2,
        grid=(nb,),
        in_specs=[tile,
                  pl.BlockSpec((1, 1, d, de), lambda i, be, nv: (layer, be[i], 0, 0)),
                  pl.BlockSpec((1, 1, d, de), lambda i, be, nv: (layer, be[i], 0, 0)),
                  pl.BlockSpec((1, 1, de, d), lambda i, be, nv: (layer, be[i], 0, 0))],
        out_specs=tile,
        scratch_shapes=[pltpu.VMEM((d, de), BF16), pltpu.VMEM((d, de), BF16), pltpu.VMEM((de, d), BF16),
                        pltpu.VMEM((MOE_BLK, d), BF16)])
    return pl.pallas_call(
        _experts_kernel,
        out_shape=jax.ShapeDtypeStruct((p, SUB, LANE), F32),
        grid_spec=grid_spec,
        compiler_params=_cparams(("arbitrary",)),
        name="moe_experts",
    )(block_e, n_valid, xp, w1, w3, w2)


def _moe_final_kernel(dest_ref, x_ref, w_ref, yp_hbm, w1_ref, w3_ref, w2_ref, lg_ref, lb_ref, o_ref,
                      buf, r_ref, sem, *, alpha, tb):
    def issue(r, c):
        for k in range(TOP_K):
            pltpu.make_async_copy(yp_hbm.at[dest_ref[r * TOP_K + k]], buf.at[k, r], sem).start()
        return c

    lax.fori_loop(0, tb, issue, 0)
    for k in range(TOP_K):
        pltpu.make_async_copy(yp_hbm.at[pl.ds(0, tb)], buf.at[k], sem).wait()
    w = w_ref[...]
    wk = [jnp.broadcast_to(w[:, k:k + 1], (tb, LANE)) for k in range(TOP_K)]
    for s in range(SUB):
        acc = buf[0, :, s, :] * wk[0]
        for k in range(1, TOP_K):
            acc = acc + buf[k, :, s, :] * wk[k]
        r_ref[:, s * LANE:(s + 1) * LANE] = acc
    x = x_ref[...]
    xb = x.astype(BF16)
    h = _silu(_dot(xb, w1_ref[...])) * _dot(xb, w3_ref[...])
    shared = _dot(h.astype(BF16), w2_ref[...])
    o_ref[...] = _ln_rows(alpha * x + (r_ref[...] + shared), lg_ref[...], lb_ref[...])


def moe_final(x, wsel, dest_flat, yp, w1, w3, w2, lg, lb, alpha):
    t, d = x.shape
    tb = _token_tile(t)
    ds = w1.shape[1]
    row = pl.BlockSpec((tb, d), lambda i: (i, 0))
    return pl.pallas_call(
        functools.partial(_moe_final_kernel, alpha=alpha, tb=tb),
        out_shape=jax.ShapeDtypeStruct((t, d), F32),
        grid=(t // tb,),
        in_specs=[pl.BlockSpec((tb * TOP_K,), lambda i: (i,), memory_space=pltpu.SMEM),
                  row, pl.BlockSpec((tb, TOP_K), lambda i: (i, 0)), pl.BlockSpec(memory_space=pl.ANY),
                  _cst((d, ds)), _cst((d, ds)), _cst((ds, d)), _cst((1, d)), _cst((1, d))],
        out_specs=row,
        scratch_shapes=[pltpu.VMEM((TOP_K, tb, SUB, LANE), F32), pltpu.VMEM((tb, d), F32),
                        pltpu.SemaphoreType.DMA(())],
        compiler_params=_cparams(("arbitrary",)),
        name="moe_final",
    )(dest_flat, x, wsel, yp, w1.astype(BF16), w3.astype(BF16), w2.astype(BF16), lg.reshape(1, d), lb.reshape(1, d))


def moe_layer(x, lw, alpha):
    n, d = x.shape
    assert d == SUB * LANE, "a token row is moved as one (8,128) tile"
    eidx_t, wsel_t, rank_t, counts = moe_router(x, lw['router_w'], lw['router_bias'])
    m = n * TOP_K
    nb = (m + N_EXPERTS * (MOE_BLK - 1) + MOE_BLK - 1) // MOE_BLK
    p = nb * MOE_BLK
    counts = counts.reshape(N_EXPERTS).astype(I32)
    pcounts = (counts + MOE_BLK - 1) // MOE_BLK * MOE_BLK
    pends = jnp.cumsum(pcounts)
    pstarts = pends - pcounts
    block_e = jnp.minimum(jnp.searchsorted(pends, jnp.arange(nb) * MOE_BLK, side='right'), N_EXPERTS - 1).astype(I32)
    n_valid = (pends[-1] // MOE_BLK).astype(I32).reshape(1)
    dest_flat = moe_dest(eidx_t, rank_t, pstarts).T.reshape(m)
    xp = moe_dispatch(x.reshape(n, SUB, LANE), dest_flat, p)
    yp = moe_experts(xp, block_e, n_valid, lw['exp_w1'], lw['exp_w3'], lw['exp_w2'], lw['layer'])
    return moe_final(x, wsel_t.T, dest_flat, yp, lw['sh_w1'], lw['sh_w3'], lw['sh_w2'],
                     lw['ln2_g'], lw['ln2_b'], alpha)


def _pack_w_in(w):
    parts, start = [], 0
    for s in IN_SIZES:
        parts.append(w[:, start:start + s])
        start += s
    gq, gk, gv, gr, glr, dq, dk, dv, iq, ik, iw, lx, ly, mq, gl = parts
    d = w.shape[0]
    padc = lambda a, n: jnp.pad(a, ((0, 0), (0, n - a.shape[1])))
    cols = [gq, gk, gv, gr, dq, dk, dv, iq, padc(glr, LANE), padc(jnp.concatenate([ik, iw], 1), LANE),
            jnp.zeros((d, Z_LX - Z_MISC - LANE), w.dtype), lx, ly, mq, gl]
    out = jnp.concatenate(cols, axis=1).astype(BF16)
    assert out.shape[1] == Z_W
    return out


def _sample_mixers(zs, hs_tabs, l, bs, ts, past, cache_k, cache_v, cache_idx_k, cache_mem_k, cache_mem_v,
                   state_gla, state_conv, state_lru, page_table, p):
    n = bs * ts
    q_bf, kd, _, vd, _, qi_bf, ki, _, wi = dsa_prep(zs, hs_tabs, p['idx_ln_g'], p['idx_ln_b'])
    o_dsa = dsa_sample(q_bf, qi_bf, wi, ki, kd, vd, cache_k, cache_v, cache_idx_k, page_table, l, bs, ts)
    glog = gla_gate(zs, p['gla_w_gate2'], p['gla_b_gate'])
    colz = lambda a: a.reshape(bs, ts, GLA_HEADS, GLA_DK).transpose(0, 2, 3, 1)
    qt = colz(zs[:, Z_GQ:Z_GQ + GLA_HEADS * GLA_DK])
    kt = colz(zs[:, Z_GK:Z_GK + GLA_HEADS * GLA_DK])
    gt = colz(glog)
    rowz = lambda off: zs[:, off:off + GLA_HEADS * GLA_DV].reshape(bs, ts, GLA_HEADS * GLA_DV)
    o_gla, s_new = gla_sample(qt, kt, gt, rowz(Z_GV), rowz(Z_GR), state_gla, l, p['gla_norm_g'])
    lx = zs[:, Z_LX:Z_LX + LRU_W].reshape(bs, ts, LRU_W)
    ly = zs[:, Z_LY:Z_LY + LRU_W].reshape(bs, ts, LRU_W)
    xp = jnp.concatenate([state_conv[l], lx], axis=1)
    o_lru, h_new = rglru_sample(xp.transpose(1, 0, 2), ly.transpose(1, 0, 2), state_lru, l,
                                p['conv_w'], p['conv_b'], p['lru_wa'], p['lru_ba'], p['lru_wx'], p['lru_bx'],
                                p['lru_lambda'], first_is_start=(past == 0))
    conv_new = xp[:, xp.shape[1] - (CONV_W - 1):]
    w = MEM_HEADS * MEM_HD
    rows = -(-ts // SQ_ROWS) * SQ_ROWS
    mq = jnp.pad(zs[:, Z_MQ:Z_MQ + w].reshape(bs, ts, w), ((0, 0), (0, rows - ts), (0, 0)))
    o_mem = mem_attend_sample(mq, cache_mem_k, cache_mem_v, l)
    o_mem = o_mem[:, :ts]
    branches = (o_gla.reshape(n, -1), o_dsa, o_lru.transpose(1, 0, 2).reshape(n, -1), o_mem.reshape(n, -1))
    states = (kd.reshape(bs, ts, DSA_KV_HEADS, DSA_HD), vd.reshape(bs, ts, DSA_KV_HEADS, DSA_HD),
              ki.reshape(bs, ts, IDX_D), s_new, conv_new, h_new)
    return branches, states


def kernel(x_prompt, x_sample, cache_k, cache_v, cache_idx_k, cache_mem_k, cache_mem_v, state_gla, state_conv, state_lru, page_table, mem_prompt, ln_in_g, ln_in_b, w_in, b_gate, gla_w_gate2, gla_b_gate, gla_norm_g, idx_ln_g, idx_ln_b, conv_w, conv_b, lru_wa, lru_ba, lru_wx, lru_bx, lru_lambda, mem_w_kv, w_branch, w_out, ln1_g, ln1_b, ln2_g, ln2_b, router_w, router_bias, exp_w1, exp_w3, exp_w2, sh_w1, sh_w3, sh_w2):
    bp, sp_len, d = x_prompt.shape
    assert bp == 1, "prompt group is a single sequence"
    bs, ts, _ = x_sample.shape
    depth = w_in.shape[0]
    alpha = (2 * depth) ** 0.25
    past = page_table.shape[1] * PAGE_SIZE
    tabs_p = rope_tables(jnp.arange(sp_len, dtype=I32))
    tabs_s = rope_tables(jnp.tile(past + jnp.arange(ts, dtype=I32), bs))

    hp = layer_norm_rows(x_prompt.reshape(sp_len, d), ln_in_g, ln_in_b)
    hs = layer_norm_rows(x_sample.reshape(bs * ts, d), ln_in_g, ln_in_b)

    st_p, st_s, mem_new = [], [], []
    for l in range(depth):
        lw = {'ln2_g': ln2_g[l], 'ln2_b': ln2_b[l], 'router_w': router_w[l], 'router_bias': router_bias[l],
              'exp_w1': exp_w1, 'exp_w3': exp_w3, 'exp_w2': exp_w2, 'layer': l,
              'sh_w1': sh_w1[l], 'sh_w3': sh_w3[l], 'sh_w2': sh_w2[l]}
        mp = {'gla_w_gate2': gla_w_gate2[l], 'gla_b_gate': gla_b_gate[l], 'gla_norm_g': gla_norm_g[l],
              'idx_ln_g': idx_ln_g[l], 'idx_ln_b': idx_ln_b[l],
              'conv_w': conv_w[l], 'conv_b': conv_b[l], 'lru_wa': lru_wa[l], 'lru_ba': lru_ba[l],
              'lru_wx': lru_wx[l], 'lru_bx': lru_bx[l], 'lru_lambda': lru_lambda[l]}
        w_in_p = _pack_w_in(w_in[l])
        wb = w_branch[l].astype(BF16)
        wo = w_out[l].astype(BF16)

        zp = matmul(hp, w_in_p)
        mkv = matmul(mem_prompt.reshape(N_MEM, d), mem_w_kv[l].astype(BF16))
        q_bf, kd, k_bf, vd, v_bf, qi_bf, ki, ki_bf, wi = dsa_prep(zp, tabs_p, idx_ln_g[l], idx_ln_b[l])
        o_gla, s_gla = gla_prompt(zp, gla_w_gate2[l], gla_b_gate[l], gla_norm_g[l])
        o_dsa = dsa_prompt(q_bf, qi_bf, wi, ki_bf.T, k_bf.T, v_bf)
        o_lru, h_last = rglru_prompt(zp, conv_w[l], conv_b[l], lru_wa[l], lru_ba[l], lru_wx[l], lru_bx[l], lru_lambda[l])
        o_mem = mem_attend_prompt(zp, mkv)
        xp1 = merge_out((o_gla, o_dsa, o_lru, o_mem), zp, hp, wb, b_gate[l], wo, ln1_g[l], ln1_b[l], alpha)
        lx_p = zp[:, Z_LX:Z_LX + LRU_W]
        conv_p = jnp.concatenate([jnp.zeros((CONV_W - 1, LRU_W), F32), lx_p], 0)[-(CONV_W - 1):]
        st_p.append((kd.reshape(1, sp_len, DSA_KV_HEADS, DSA_HD), vd.reshape(1, sp_len, DSA_KV_HEADS, DSA_HD),
                     ki.reshape(1, sp_len, IDX_D), s_gla[None], conv_p[None], h_last.reshape(1, LRU_W)))
        mem_new.append((mkv[:, :MEM_HEADS * MEM_HD].reshape(1, N_MEM, MEM_HEADS, MEM_HD),
                        mkv[:, MEM_HEADS * MEM_HD:].reshape(1, N_MEM, MEM_HEADS, MEM_HD)))

        zs = matmul(hs, w_in_p)
        brs, sts = _sample_mixers(zs, tabs_s, l, bs, ts, past, cache_k, cache_v, cache_idx_k, cache_mem_k,
                                  cache_mem_v, state_gla, state_conv, state_lru, page_table, mp)
        xs1 = merge_out(brs, zs, hs, wb, b_gate[l], wo, ln1_g[l], ln1_b[l], alpha)
        st_s.append(sts)

        x_all = moe_layer(jnp.concatenate([xp1, xs1], 0), lw, alpha)
        hp, hs = x_all[:sp_len], x_all[sp_len:]

    stk = lambda seq, j: jnp.stack([t[j] for t in seq], axis=0)
    k_p, v_p, ik_p, gla_p, conv_p_, lru_p = [stk(st_p, j) for j in range(6)]
    k_s, v_s, ik_s, gla_s, conv_s, lru_s = [stk(st_s, j) for j in range(6)]
    memk_p, memv_p = stk(mem_new, 0), stk(mem_new, 1)
    return (hp.reshape(1, sp_len, d), hs.reshape(bs, ts, d), k_p, v_p, ik_p, gla_p, conv_p_, lru_p, memk_p, memv_p,
            k_s, v_s, ik_s, gla_s, conv_s, lru_s)
```

```python
import functools
import math

import jax
import jax.numpy as jnp
from jax import lax
from jax.experimental import pallas as pl
from jax.experimental.pallas import tpu as pltpu

F32 = jnp.float32
BF16 = jnp.bfloat16
I32 = jnp.int32

D_MODEL = 1024
PAGE_SIZE = 128
N_BRANCH = 4
GLA_HEADS = 4
GLA_DK = 128
GLA_DV = 256
GLA_RANK = 16
GLA_TAU = 16.0
GLA_CHUNK = 64
DSA_HEADS = 8
DSA_KV_HEADS = 4
DSA_HD = 128
DSA_G = DSA_HEADS // DSA_KV_HEADS
IDX_HEADS = 8
IDX_D = 64
TOPK_MAX = 256
LRU_W = 1024
LRU_BLOCKS = 4
LRU_BW = LRU_W // LRU_BLOCKS
CONV_W = 4
LRU_C = 8.0
N_MEM = 256
MEM_HEADS = 4
MEM_HD = 256
N_EXPERTS = 256
TOP_K = 8
N_GROUPS = 8
TOPK_GROUPS = 4
ROUTED_SCALE = 2.5
ROPE_THETA = 10000.0
LN_EPS = 1e-5

IN_SIZES = (GLA_HEADS * GLA_DK, GLA_HEADS * GLA_DK, GLA_HEADS * GLA_DV, GLA_HEADS * GLA_DV, GLA_RANK,
            DSA_HEADS * DSA_HD, DSA_KV_HEADS * DSA_HD, DSA_KV_HEADS * DSA_HD,
            IDX_HEADS * IDX_D, IDX_D, IDX_HEADS,
            LRU_W, LRU_W, MEM_HEADS * MEM_HD, N_BRANCH * D_MODEL)

LANE = 128
SUB = 8
Z_GQ, Z_GK, Z_GV, Z_GR = 0, 512, 1024, 2048
Z_DQ, Z_DK, Z_DV, Z_IQ = 3072, 4096, 4608, 5120
Z_GLR, Z_MISC = 5632, 5760
Z_LX, Z_LY, Z_MQ, Z_GL = 6144, 7168, 8192, 9216
Z_W = 13312

VMEM_LIMIT = 56 * 1024 * 1024
MOE_BLK = 256
INT_MIN = -2 ** 31
NEG_BIG = -1e30
LOG2E = 1.4426950408889634
SQ_ROWS = 2 * SUB


def _tile(n, target):
    if n <= target:
        return n
    for t in range(target, 7, -1):
        if n % t == 0 and t % 8 == 0:
            return t
    return n


def _cparams(sem, vmem=None):
    return pltpu.CompilerParams(dimension_semantics=sem, vmem_limit_bytes=vmem or VMEM_LIMIT)


def _ln_rows(x, g, b):
    mu = jnp.mean(x, -1, keepdims=True)
    xc = x - mu
    var = jnp.mean(xc * xc, -1, keepdims=True)
    return xc * lax.rsqrt(var + LN_EPS) * g + b


def _sigmoid(x):
    return 1.0 / (1.0 + jnp.exp(-x))


def _silu(x):
    return x * _sigmoid(x)


def _dot(a, b):
    return jnp.dot(a, b, preferred_element_type=F32)


def _dot_nt(a, b):
    return lax.dot_general(a, b, (((1,), (1,)), ((), ())), preferred_element_type=F32)


def _dot_tn(a, b):
    return lax.dot_general(a, b, (((0,), (0,)), ((), ())), preferred_element_type=F32)


def _cst(shape):
    return pl.BlockSpec(shape, lambda *_: tuple(0 for _ in shape))


def _ln_kernel(x_ref, g_ref, b_ref, o_ref):
    o_ref[...] = _ln_rows(x_ref[...], g_ref[...], b_ref[...])


def layer_norm_rows(x, g, b):
    m, d = x.shape
    tm = _tile(m, 512)
    return pl.pallas_call(
        _ln_kernel,
        out_shape=jax.ShapeDtypeStruct((m, d), F32),
        grid=(m // tm,),
        in_specs=[pl.BlockSpec((tm, d), lambda i: (i, 0)), _cst((1, d)), _cst((1, d))],
        out_specs=pl.BlockSpec((tm, d), lambda i: (i, 0)),
        compiler_params=_cparams(("parallel",)),
        name="ln_rows",
    )(x, g.reshape(1, d), b.reshape(1, d))


def _mm_kernel(x_ref, w_ref, o_ref, xb_ref):
    @pl.when(pl.program_id(1) == 0)
    def _():
        xb_ref[...] = x_ref[...].astype(BF16)

    o_ref[...] = _dot(xb_ref[...], w_ref[...])


def matmul(x, w):
    m, k = x.shape
    n = w.shape[1]
    tm = _tile(m, 1024)
    tn = _tile(n, 1024)
    return pl.pallas_call(
        _mm_kernel,
        out_shape=jax.ShapeDtypeStruct((m, n), F32),
        grid=(m // tm, n // tn),
        in_specs=[pl.BlockSpec((tm, k), lambda i, j: (i, 0)),
                  pl.BlockSpec((k, tn), lambda i, j: (0, j))],
        out_specs=pl.BlockSpec((tm, tn), lambda i, j: (i, j)),
        scratch_shapes=[pltpu.VMEM((tm, k), BF16)],
        compiler_params=_cparams(("parallel", "arbitrary")),
        name="matmul",
    )(x, w)


def _rot_half(x, width):
    if width == LANE:
        return pltpu.roll(x, LANE // 2, 1)
    half = width // 2
    lane = lax.broadcasted_iota(I32, x.shape, 1)
    first = (lane % width) < half
    return jnp.where(first, pltpu.roll(x, LANE - half, 1), pltpu.roll(x, half, 1))


def _prep_kernel(dq_ref, dk_ref, dv_ref, iq_ref, misc_ref, c128_ref, s128_ref, c64_ref, s64_ref,
                 ig_ref, ib_ref,
                 q_ref, k_ref, kb_ref, v_ref, vb_ref, qi_ref, ki_ref, kib_ref, wi_ref):
    c128, s128 = c128_ref[...], s128_ref[...]
    c64, s64 = c64_ref[...], s64_ref[...]
    qscale = (DSA_HD ** -0.5) * LOG2E
    for h in range(DSA_HEADS):
        sl = slice(h * LANE, (h + 1) * LANE)
        x = dq_ref[:, sl]
        q_ref[:, sl] = ((x * c128 + _rot_half(x, LANE) * s128) * qscale).astype(BF16)
    for h in range(DSA_KV_HEADS):
        sl = slice(h * LANE, (h + 1) * LANE)
        x = dk_ref[:, sl]
        r = x * c128 + _rot_half(x, LANE) * s128
        k_ref[:, sl] = r
        kb_ref[:, sl] = r.astype(BF16)
    v = dv_ref[...]
    v_ref[...] = v
    vb_ref[...] = v.astype(BF16)
    for c in range(IDX_HEADS * IDX_D // LANE):
        sl = slice(c * LANE, (c + 1) * LANE)
        x = iq_ref[:, sl]
        qi_ref[:, sl] = (x * c64 + _rot_half(x, IDX_D) * s64).astype(BF16)
    misc = misc_ref[...]
    lane = lax.broadcasted_iota(I32, misc.shape, 1)
    isk = lane < IDX_D
    mu = jnp.sum(jnp.where(isk, misc, 0.0), -1, keepdims=True) * (1.0 / IDX_D)
    xc = jnp.where(isk, misc - mu, 0.0)
    var = jnp.sum(xc * xc, -1, keepdims=True) * (1.0 / IDX_D)
    kn = xc * lax.rsqrt(var + LN_EPS) * ig_ref[...] + ib_ref[...]
    kr = kn * c64 + _rot_half(kn, IDX_D) * s64
    ki_ref[...] = kr[:, :IDX_D]
    kib_ref[...] = kr[:, :IDX_D].astype(BF16)
    w = pltpu.roll(misc, LANE - IDX_D, 1)
    wi_ref[...] = jnp.where(lane < IDX_HEADS, w * (IDX_HEADS ** -0.5) * (IDX_D ** -0.5), 0.0)


def dsa_prep(z, tabs, idx_g, idx_b):
    t = z.shape[0]
    tb = _tile(t, 512)
    c128, s128, c64, s64 = tabs
    zspec = lambda w, off: pl.BlockSpec((tb, w), lambda i, o=off // w: (i, o))
    row = lambda w: pl.BlockSpec((tb, w), lambda i: (i, 0))
    pad = lambda a: jnp.pad(a, (0, LANE - IDX_D)).reshape(1, LANE)
    return pl.pallas_call(
        _prep_kernel,
        out_shape=[jax.ShapeDtypeStruct((t, 1024), BF16),
                   jax.ShapeDtypeStruct((t, 512), F32),
                   jax.ShapeDtypeStruct((t, 512), BF16),
                   jax.ShapeDtypeStruct((t, 512), F32),
                   jax.ShapeDtypeStruct((t, 512), BF16),
                   jax.ShapeDtypeStruct((t, 512), BF16),
                   jax.ShapeDtypeStruct((t, IDX_D), F32),
                   jax.ShapeDtypeStruct((t, IDX_D), BF16),
                   jax.ShapeDtypeStruct((t, LANE), F32)],
        grid=(t // tb,),
        in_specs=[zspec(1024, Z_DQ), zspec(512, Z_DK), zspec(512, Z_DV), zspec(512, Z_IQ), zspec(LANE, Z_MISC),
                  row(LANE), row(LANE), row(LANE), row(LANE), _cst((1, LANE)), _cst((1, LANE))],
        out_specs=[row(1024), row(512), row(512), row(512), row(512), row(512), row(IDX_D), row(IDX_D), row(LANE)],
        compiler_params=_cparams(("parallel",)),
        name="dsa_prep",
    )(z, z, z, z, z, c128, s128, c64, s64, pad(idx_g), pad(idx_b))


def rope_tables(pos):
    posf = pos.astype(F32)[:, None]

    def tab(width):
        half = width // 2
        inv = ROPE_THETA ** (-jnp.arange(half, dtype=F32) / half)
        ang = posf * inv
        c, s = jnp.cos(ang), jnp.sin(ang)
        reps = LANE // width
        return jnp.tile(jnp.concatenate([c, c], 1), (1, reps)), jnp.tile(jnp.concatenate([-s, s], 1), (1, reps))

    c128, s128 = tab(DSA_HD)
    c64, s64 = tab(IDX_D)
    return c128, s128, c64, s64


def _log_sigmoid(x):
    return jnp.minimum(x, 0.0) - jnp.log1p(jnp.exp(-jnp.abs(x)))


def _gla_kernel(q_ref, k_ref, v_ref, r_ref, glr_ref, w2_ref, bg_ref, ng_ref, o_ref, st_ref, s_ref, *, tb):
    c = GLA_CHUNK
    j = pl.program_id(1)

    @pl.when(j == 0)
    def _():
        s_ref[...] = jnp.zeros_like(s_ref)

    ri = lax.broadcasted_iota(I32, (c, c), 0)
    ci = lax.broadcasted_iota(I32, (c, c), 1)
    causal = ci <= ri
    tri = causal.astype(F32)
    w2 = w2_ref[...]
    bg = bg_ref[...]
    ng = ng_ref[...]

    def chunk(ic, carry):
        r0 = pl.multiple_of(ic * c, c)
        q = q_ref[pl.ds(r0, c), :] * (GLA_DK ** -0.5)
        k = k_ref[pl.ds(r0, c), :]
        v = v_ref[pl.ds(r0, c), :].astype(BF16)
        x = _dot(glr_ref[pl.ds(r0, c), :].astype(BF16), w2) + bg
        g = _log_sigmoid(x) * (1.0 / GLA_TAU)
        b = jnp.dot(tri, g, preferred_element_type=F32, precision=lax.Precision.HIGHEST)
        bm = b[c // 2 - 1:c // 2, :]
        bl = b[c - 1:c, :]
        st = s_ref[...]
        qe = (q * jnp.exp(b - bm)).astype(BF16)
        ke = (k * jnp.exp(bm - b)).astype(BF16)
        a = jnp.where(causal, _dot_nt(qe, ke), 0.0)
        o = _dot_nt((q * jnp.exp(b)).astype(BF16), st.astype(BF16)) + _dot(a.astype(BF16), v)
        kd = (k * jnp.exp(bl - b)).astype(BF16)
        s_ref[...] = st * jnp.exp(bl) + _dot_tn(v, kd)
        on = o * lax.rsqrt(jnp.mean(o * o, -1, keepdims=True) + LN_EPS) * ng
        o_ref[pl.ds(r0, c), :] = on * _silu(r_ref[pl.ds(r0, c), :])
        return carry

    lax.fori_loop(0, tb // c, chunk, 0)

    @pl.when(j == pl.num_programs(1) - 1)
    def _():
        st_ref[0] = s_ref[...]


def gla_prompt(z, w2, bg, ng):
    t = z.shape[0]
    tb = _tile(t, 512)
    assert tb % GLA_CHUNK == 0
    h = GLA_HEADS
    w2p = jnp.pad(w2, ((0, LANE - GLA_RANK), (0, 0))).astype(BF16)
    o, st = pl.pallas_call(
        functools.partial(_gla_kernel, tb=tb),
        out_shape=[jax.ShapeDtypeStruct((t, h * GLA_DV), F32),
                   jax.ShapeDtypeStruct((h, GLA_DV, GLA_DK), F32)],
        grid=(h, t // tb),
        in_specs=[pl.BlockSpec((tb, GLA_DK), lambda hh, j: (j, Z_GQ // GLA_DK + hh)),
                  pl.BlockSpec((tb, GLA_DK), lambda hh, j: (j, Z_GK // GLA_DK + hh)),
                  pl.BlockSpec((tb, GLA_DV), lambda hh, j: (j, Z_GV // GLA_DV + hh)),
                  pl.BlockSpec((tb, GLA_DV), lambda hh, j: (j, Z_GR // GLA_DV + hh)),
                  pl.BlockSpec((tb, LANE), lambda hh, j: (j, Z_GLR // LANE)),
                  pl.BlockSpec((LANE, GLA_DK), lambda hh, j: (0, hh)),
                  pl.BlockSpec((1, GLA_DK), lambda hh, j: (0, hh)),
                  pl.BlockSpec((1, GLA_DV), lambda hh, j: (0, 0))],
        out_specs=[pl.BlockSpec((tb, GLA_DV), lambda hh, j: (j, hh)),
                   pl.BlockSpec((1, GLA_DV, GLA_DK), lambda hh, j: (hh, 0, 0))],
        scratch_shapes=[pltpu.VMEM((GLA_DV, GLA_DK), F32)],
        compiler_params=_cparams(("parallel", "arbitrary")),
        name="gla_prompt",
    )(z, z, z, z, z, w2p, bg.reshape(1, -1), ng.reshape(1, -1))
    return o, jnp.swapaxes(st, 1, 2)


def _sort_key(x):
    bits = lax.bitcast_convert_type(x + 0.0, I32)
    return bits ^ ((bits >> 31) & 0x7FFFFFFF)


def _kth_largest_key(count_ge, k_sel):
    kf = float(k_sel)
    c_adm = count_ge(INT_MIN + 1)
    c0 = count_ge(0)
    nonneg = c0 >= kf
    lo = jnp.where(nonneg, 0, INT_MIN + 1).astype(I32)
    cnt = jnp.where(nonneg, c0, c_adm)

    def cond(st):
        return (st[0] < 31) & st[3]

    def bit_step(st):
        it, lo, cnt, _ = st
        cand = lo + jnp.left_shift(jnp.int32(1), 30 - it)
        c = count_ge(cand)
        take = c >= kf
        cnt = jnp.where(take, c, cnt)
        return it + 1, jnp.where(take, cand, lo), cnt, jnp.max(cnt) > kf

    _, tau, n_ge, _ = lax.while_loop(cond, bit_step, (jnp.int32(0), lo, cnt, jnp.max(cnt) > kf))
    n_gt = count_ge(tau + 1)
    need = kf - n_gt
    tie = n_ge - kf
    return tau, need, tie


def _dsa_kernel(q_ref, qi_ref, wi_ref, kit_ref, kt_ref, v_ref, o_ref,
                key_ref, tri_ref, q2_ref, wb_ref, m_ref, l_ref, acc_ref, *, tq, tk, ts, ta, k_sel):
    i = pl.program_id(0)

    @pl.when(i == 0)
    def _():
        r = lax.broadcasted_iota(I32, (tk, tk), 0)
        c = lax.broadcasted_iota(I32, (tk, tk), 1)
        tri_ref[...] = (r <= c).astype(BF16)

    t0 = i * tq
    n_kt = (t0 + tq + tk - 1) // tk
    wi = wi_ref[...]
    for n in range(DSA_KV_HEADS):
        for g in range(DSA_G):
            h = n * DSA_G + g
            q2_ref[n, g * tq:(g + 1) * tq, :] = q_ref[:, h * DSA_HD:(h + 1) * DSA_HD]

    row_s = t0 + lax.broadcasted_iota(I32, (tq, ts), 0)

    for h in range(IDX_HEADS):
        wb_ref[h] = jnp.broadcast_to(wi[:, h:h + 1], (tq, LANE))

    def score_chunk(kc, carry):
        c0 = pl.multiple_of(kc * ts, ts)
        acc = jnp.zeros((tq, ts), F32)
        for h in range(IDX_HEADS):
            sc = _dot(qi_ref[:, h * IDX_D:(h + 1) * IDX_D], kit_ref[:, pl.ds(c0, ts)])
            acc = acc + jnp.maximum(sc, 0.0) * jnp.concatenate([wb_ref[h]] * (ts // LANE), axis=1)
        col = c0 + lax.broadcasted_iota(I32, (tq, ts), 1)
        key_ref[:, pl.ds(c0, ts)] = jnp.where(col <= row_s, _sort_key(acc), INT_MIN)
        return carry

    lax.fori_loop(0, n_kt * (tk // ts), score_chunk, 0)

    nl = tk // LANE

    def count_ge(cand):
        cb = jnp.broadcast_to(cand, (tq, LANE))

        def body(kt, acc):
            c0 = pl.multiple_of(kt * tk, tk)
            for j in range(nl):
                acc = acc + jnp.where(key_ref[:, pl.ds(c0 + j * LANE, LANE)] >= cb, 1, 0)
            return acc

        acc = lax.fori_loop(0, n_kt, body, jnp.zeros((tq, LANE), I32))
        return jnp.sum(acc.astype(F32), axis=1, keepdims=True)

    tau, need, tie = _kth_largest_key(count_ge, k_sel)
    tau_eff = jnp.maximum(tau, INT_MIN + 1)
    has_tie = jnp.max(tie) > 0.0

    m_ref[...] = jnp.full_like(m_ref, NEG_BIG)
    l_ref[...] = jnp.zeros_like(l_ref)
    acc_ref[...] = jnp.zeros_like(acc_ref)

    def attend(kt, run_eq):
        c0 = pl.multiple_of(kt * tk, tk)
        keys = key_ref[:, pl.ds(c0, tk)]

        def fast(run_eq):
            return jnp.where(keys >= tau_eff, 0.0, NEG_BIG), run_eq

        def slow(run_eq):
            eq = (keys == tau) & (keys > INT_MIN)
            rank = run_eq + _dot(eq.astype(BF16), tri_ref[...])
            sel = (keys > tau) | (eq & (rank <= need))
            return jnp.where(sel, 0.0, NEG_BIG), run_eq + jnp.sum(eq.astype(F32), axis=1, keepdims=True)

        bias, run_eq = lax.cond(has_tie, slow, fast, run_eq)
        for a in range(tk // ta):
            ca = pl.multiple_of(c0 + a * ta, ta)
            bias_a = bias[:, a * ta:(a + 1) * ta][None]
            ps, alphas = [], []
            for n in range(DSA_KV_HEADS):
                ktile = kt_ref[n * DSA_HD:(n + 1) * DSA_HD, pl.ds(ca, ta)]
                s = (_dot(q2_ref[n], ktile).reshape(DSA_G, tq, ta) + bias_a).reshape(DSA_G * tq, ta)
                m_old = m_ref[n]
                m_new = jnp.maximum(m_old, jnp.max(s, axis=1, keepdims=True))
                p = jnp.exp2(s - m_new)
                alpha = jnp.exp2(m_old - m_new)
                l_ref[n] = alpha * l_ref[n] + jnp.sum(p, axis=1, keepdims=True)
                m_ref[n] = m_new
                ps.append(p.astype(BF16))
                alphas.append(alpha)
            for n in range(DSA_KV_HEADS):
                vtile = v_ref[pl.ds(ca, ta), n * DSA_HD:(n + 1) * DSA_HD]
                acc_ref[n] = alphas[n] * acc_ref[n] + _dot(ps[n], vtile)
        return run_eq

    lax.fori_loop(0, n_kt, attend, jnp.zeros((tq, 1), F32))
    for n in range(DSA_KV_HEADS):
        o = acc_ref[n] / l_ref[n]
        for g in range(DSA_G):
            h = n * DSA_G + g
            o_ref[:, h * DSA_HD:(h + 1) * DSA_HD] = o[g * tq:(g + 1) * tq, :]


def dsa_prompt(q_bf, qi_bf, wi, kit_bf, kt_bf, v_bf):
    t = q_bf.shape[0]
    tq = _tile(t, 128)
    tk = _tile(t, 1024)
    ts = tk
    ta = tk
    k_sel = min(TOPK_MAX, t // 4)
    whole = lambda shape: pl.BlockSpec(shape, lambda i: (0, 0), pipeline_mode=pl.Buffered(1))
    return pl.pallas_call(
        functools.partial(_dsa_kernel, tq=tq, tk=tk, ts=ts, ta=ta, k_sel=k_sel),
        out_shape=jax.ShapeDtypeStruct((t, DSA_HEADS * DSA_HD), F32),
        grid=(t // tq,),
        in_specs=[pl.BlockSpec((tq, DSA_HEADS * DSA_HD), lambda i: (i, 0)),
                  pl.BlockSpec((tq, IDX_HEADS * IDX_D), lambda i: (i, 0)),
                  pl.BlockSpec((tq, LANE), lambda i: (i, 0)),
                  whole((IDX_D, t)), whole((DSA_KV_HEADS * DSA_HD, t)), whole((t, DSA_KV_HEADS * DSA_HD))],
        out_specs=pl.BlockSpec((tq, DSA_HEADS * DSA_HD), lambda i: (i, 0)),
        scratch_shapes=[pltpu.VMEM((tq, t), I32),
                        pltpu.VMEM((tk, tk), BF16),
                        pltpu.VMEM((DSA_KV_HEADS, DSA_G * tq, DSA_HD), BF16),
                        pltpu.VMEM((IDX_HEADS, tq, LANE), F32),
                        pltpu.VMEM((DSA_KV_HEADS, DSA_G * tq, 1), F32),
                        pltpu.VMEM((DSA_KV_HEADS, DSA_G * tq, 1), F32),
                        pltpu.VMEM((DSA_KV_HEADS, DSA_G * tq, DSA_HD), F32)],
        compiler_params=_cparams(("arbitrary",)),
        name="dsa_prompt",
    )(q_bf, qi_bf, wi, kit_bf, kt_bf, v_bf)


def _sdsa_select_kernel(pt_ref, qi_ref, w_ref, kn_ref, *rest, ppc, ts, k_sel, past):
    pages = rest[:ppc]
    keys_out, tau_out, need_out, tie_out = rest[ppc:ppc + 4]
    key_ref, kc_ref = rest[ppc + 4:]
    c = pl.program_id(1)
    nch = pl.num_programs(1)
    qi = qi_ref[0]
    w = w_ref[0]

    def token_scores(sc):
        x = jnp.maximum(sc, 0.0) * w
        parts = [jnp.sum(x[t * IDX_HEADS:(t + 1) * IDX_HEADS], axis=0, keepdims=True) for t in range(ts)]
        return jnp.concatenate(parts + [jnp.zeros((SUB - ts, sc.shape[1]), F32)], axis=0)

    for j in range(ppc):
        kc_ref[j * PAGE_SIZE:(j + 1) * PAGE_SIZE, :] = pages[j][0, 0].astype(BF16)
    width = ppc * PAGE_SIZE
    score = token_scores(_dot_nt(qi, kc_ref[...]))
    rows = lax.broadcasted_iota(I32, (SUB, width), 0)
    key_ref[:, pl.ds(pl.multiple_of(c * width, width), width)] = jnp.where(rows < ts, _sort_key(score), INT_MIN)

    @pl.when(c == nch - 1)
    def _():
        kn = jnp.concatenate([kn_ref[0], jnp.zeros((LANE - ts, IDX_D), F32)], axis=0).astype(BF16)
        sc = token_scores(_dot_nt(qi, kn))
        r8 = lax.broadcasted_iota(I32, (SUB, LANE), 0)
        l8 = lax.broadcasted_iota(I32, (SUB, LANE), 1)
        ok = (r8 < ts) & (l8 <= r8)
        key_ref[:, past:past + LANE] = jnp.where(ok, _sort_key(sc), INT_MIN)

        def count_ge(cand):
            return jnp.sum(jnp.where(key_ref[...] >= cand, 1.0, 0.0), axis=1, keepdims=True)

        tau, need, tie = _kth_largest_key(count_ge, k_sel)
        keys_out[0] = key_ref[...]
        tau_out[0] = jnp.broadcast_to(tau, (SUB, LANE))
        need_out[0] = jnp.broadcast_to(need, (SUB, LANE))
        tie_out[0] = jnp.broadcast_to(tie, (SUB, LANE))


def _sdsa_attend_kernel(pt_ref, q_ref, keys_ref, ktail_ref, tau_ref, need_ref, tie_ref, kn_ref, vn_ref, *rest,
                        ppc, ts):
    kpages = rest[:ppc]
    vpages = rest[ppc:2 * ppc]
    o_ref = rest[2 * ppc]
    m_ref, l_ref, acc_ref, run_ref, tri_ref, ex_ref, kc_ref, vc_ref = rest[2 * ppc + 1:]
    c = pl.program_id(1)
    nch = pl.num_programs(1)
    width = ppc * PAGE_SIZE
    nq = DSA_KV_HEADS * SQ_ROWS
    prow = PAGE_SIZE * DSA_KV_HEADS

    @pl.when(c == 0)
    def _():
        m_ref[...] = jnp.full_like(m_ref, NEG_BIG)
        l_ref[...] = jnp.zeros_like(l_ref)
        acc_ref[...] = jnp.zeros_like(acc_ref)
        run_ref[...] = jnp.zeros_like(run_ref)
        r = lax.broadcasted_iota(I32, (width, width), 0)
        cc = lax.broadcasted_iota(I32, (width, width), 1)
        tri_ref[...] = (r <= cc).astype(BF16)
        er = lax.broadcasted_iota(I32, (PAGE_SIZE, prow), 0)
        ec = lax.broadcasted_iota(I32, (PAGE_SIZE, prow), 1)
        ex_ref[...] = ((ec >= er * DSA_KV_HEADS) & (ec < (er + 1) * DSA_KV_HEADS)).astype(BF16)

    tau = tau_ref[0][:, 0:1]
    need = need_ref[0][:, 0:1]
    tau_eff = jnp.maximum(tau, INT_MIN + 1)
    has_tie = jnp.max(tie_ref[0]) > 0.0
    q = q_ref[0]

    def process(keys, kb, vb):
        w = keys.shape[1]
        wi_ = w * DSA_KV_HEADS

        def fast(run_eq):
            return jnp.where(keys >= tau_eff, 1.0, 0.0), run_eq

        def slow(run_eq):
            eq = (keys == tau) & (keys > INT_MIN)
            rank = run_eq + _dot(eq.astype(BF16), tri_ref[0:w, 0:w])
            sel = (keys > tau) | (eq & (rank <= need))
            return jnp.where(sel, 1.0, 0.0), run_eq + jnp.sum(eq.astype(F32), axis=1, keepdims=True)

        sel01, run_eq = lax.cond(has_tie, slow, fast, run_ref[...])
        run_ref[...] = run_eq
        selb = sel01.astype(BF16)
        sel_x = jnp.concatenate([_dot(selb[:, j * PAGE_SIZE:(j + 1) * PAGE_SIZE], ex_ref[...])
                                 for j in range(w // PAGE_SIZE)], axis=1)
        bias_t = (sel_x - 1.0) * (-NEG_BIG)
        col = lax.broadcasted_iota(I32, (SQ_ROWS, wi_), 1)
        bias_t2 = jnp.concatenate([bias_t] * DSA_G, axis=0)
        bias_all = jnp.concatenate(
            [bias_t2 + jnp.where(col % DSA_KV_HEADS == n, 0.0, NEG_BIG) for n in range(DSA_KV_HEADS)], axis=0)
        s = _dot_nt(q, kb) + bias_all
        m_old = m_ref[...]
        m_new = jnp.maximum(m_old, jnp.max(s, axis=1, keepdims=True))
        p = jnp.exp2(s - m_new)
        alpha = jnp.exp2(m_old - m_new)
        l_ref[...] = alpha * l_ref[...] + jnp.sum(p, axis=1, keepdims=True)
        acc_ref[...] = alpha * acc_ref[...] + _dot(p.astype(BF16), vb)
        m_ref[...] = m_new

    for j in range(ppc):
        kc_ref[j * prow:(j + 1) * prow, :] = kpages[j][0, 0].astype(BF16)
        vc_ref[j * prow:(j + 1) * prow, :] = vpages[j][0, 0].astype(BF16)
    process(keys_ref[0], kc_ref[...], vc_ref[...])

    @pl.when(c == nch - 1)
    def _():
        pad = jnp.zeros((prow - ts * DSA_KV_HEADS, DSA_HD), F32)
        kn = jnp.concatenate([kn_ref[0], pad], axis=0).astype(BF16)
        vn = jnp.concatenate([vn_ref[0], pad], axis=0).astype(BF16)
        process(ktail_ref[0], kn, vn)
        o_ref[0] = acc_ref[...] / l_ref[...]


def dsa_sample(q_bf, qi_bf, wi, ki_new, kd_new, vd_new, cache_k, cache_v, cache_idx_k, page_table, layer, bs, ts):
    assert ts <= SUB
    n_pages = page_table.shape[1]
    past = n_pages * PAGE_SIZE
    ppc = 8 if n_pages % 8 == 0 else n_pages
    nch = n_pages // ppc
    width = ppc * PAGE_SIZE
    nk = past + LANE
    k_sel = min(TOPK_MAX, (past + ts) // 4)
    dkv = DSA_KV_HEADS * DSA_HD
    pt = page_table.reshape(-1).astype(I32)
    qi_r = qi_bf.reshape(bs, ts * IDX_HEADS, IDX_D)
    w_r = wi[:, :IDX_HEADS].reshape(bs, ts * IDX_HEADS, 1)
    cik = cache_idx_k

    def page_spec(tail, j, rows=PAGE_SIZE):
        return pl.BlockSpec((1, 1, rows) + tail,
                            lambda b, c, ptr, j=j: (layer, ptr[b * n_pages + c * ppc + j]) + (0,) * (1 + len(tail)))

    per_b = lambda shape: pl.BlockSpec((1,) + shape, lambda b, c, ptr: (b,) + tuple(0 for _ in shape))
    sel_spec = pltpu.PrefetchScalarGridSpec(
        num_scalar_prefetch=1, grid=(bs, nch),
        in_specs=[per_b((ts * IDX_HEADS, IDX_D)), per_b((ts * IDX_HEADS, 1)), per_b((ts, IDX_D))]
                 + [page_spec((IDX_D,), j) for j in range(ppc)],
        out_specs=[per_b((SUB, nk)), per_b((SUB, LANE)), per_b((SUB, LANE)), per_b((SUB, LANE))],
        scratch_shapes=[pltpu.VMEM((SUB, nk), I32), pltpu.VMEM((width, IDX_D), BF16)])
    keys, tau, need, tie = pl.pallas_call(
        functools.partial(_sdsa_select_kernel, ppc=ppc, ts=ts, k_sel=k_sel, past=past),
        out_shape=[jax.ShapeDtypeStruct((bs, SUB, nk), I32), jax.ShapeDtypeStruct((bs, SUB, LANE), I32),
                   jax.ShapeDtypeStruct((bs, SUB, LANE), F32), jax.ShapeDtypeStruct((bs, SUB, LANE), F32)],
        grid_spec=sel_spec,
        compiler_params=_cparams(("parallel", "arbitrary")),
        name="dsa_sample_select",
    )(pt, qi_r, w_r, ki_new.reshape(bs, ts, IDX_D), *([cik] * ppc))

    qx = q_bf.reshape(bs, ts, DSA_KV_HEADS, DSA_G, DSA_HD).transpose(0, 2, 3, 1, 4)
    nq = DSA_KV_HEADS * SQ_ROWS
    qx = jnp.pad(qx, ((0, 0), (0, 0), (0, 0), (0, SUB - ts), (0, 0))).reshape(bs, nq, DSA_HD)
    prow = PAGE_SIZE * DSA_KV_HEADS
    rows_view = lambda a: a.reshape(a.shape[0], a.shape[1], prow, DSA_HD)
    att_spec = pltpu.PrefetchScalarGridSpec(
        num_scalar_prefetch=1, grid=(bs, nch),
        in_specs=[per_b((nq, DSA_HD)),
                  pl.BlockSpec((1, SUB, width), lambda b, c, ptr: (b, 0, c)),
                  pl.BlockSpec((1, SUB, LANE), lambda b, c, ptr: (b, 0, past // LANE)),
                  per_b((SUB, LANE)), per_b((SUB, LANE)), per_b((SUB, LANE)),
                  per_b((ts * DSA_KV_HEADS, DSA_HD)), per_b((ts * DSA_KV_HEADS, DSA_HD))]
                 + [page_spec((DSA_HD,), j % ppc, prow) for j in range(2 * ppc)],
        out_specs=per_b((nq, DSA_HD)),
        scratch_shapes=[pltpu.VMEM((nq, 1), F32), pltpu.VMEM((nq, 1), F32), pltpu.VMEM((nq, DSA_HD), F32),
                        pltpu.VMEM((SUB, 1), F32), pltpu.VMEM((width, width), BF16),
                        pltpu.VMEM((PAGE_SIZE, prow), BF16),
                        pltpu.VMEM((ppc * prow, DSA_HD), BF16), pltpu.VMEM((ppc * prow, DSA_HD), BF16)])
    o = pl.pallas_call(
        functools.partial(_sdsa_attend_kernel, ppc=ppc, ts=ts),
        out_shape=jax.ShapeDtypeStruct((bs, nq, DSA_HD), F32),
        grid_spec=att_spec,
        compiler_params=_cparams(("parallel", "arbitrary")),
        name="dsa_sample_attend",
    )(pt, qx, keys, keys, tau, need, tie,
      kd_new.reshape(bs, ts * DSA_KV_HEADS, DSA_HD), vd_new.reshape(bs, ts * DSA_KV_HEADS, DSA_HD),
      *([rows_view(cache_k)] * ppc), *([rows_view(cache_v)] * ppc))
    o = o.reshape(bs, DSA_KV_HEADS, DSA_G, SUB, DSA_HD)[:, :, :, :ts]
    return o.transpose(0, 3, 1, 2, 4).reshape(bs * ts, DSA_HEADS * DSA_HD)


def _gelu_tanh(x):
    return 0.5 * x * (1.0 + jnp.tanh(math.sqrt(2.0 / math.pi) * (x + 0.044715 * x * x * x)))


def _softplus(x):
    return jnp.maximum(x, 0.0) + jnp.log1p(jnp.exp(-jnp.abs(x)))


def _lru_gates(xc, wa, ba, wx, bx, lam):
    xcb = xc.astype(BF16)
    gate_r = _sigmoid(_dot(xcb, wa) + ba)
    gate_i = _sigmoid(_dot(xcb, wx) + bx)
    log_a = -LRU_C * gate_r * _softplus(-lam)
    th = jnp.tanh(log_a)
    mult = jnp.sqrt(-2.0 * th / (1.0 - th))
    return jnp.exp(log_a), mult, gate_i * xc


def _lru_kernel(x_ref, y_ref, cw_ref, cb_ref, wa_ref, ba_ref, wx_ref, bx_ref, lam_ref,
                o_ref, hl_ref, xp_ref, h_ref, *, tb):
    i = pl.program_id(0)

    @pl.when(i == 0)
    def _():
        xp_ref[0:8, :] = jnp.zeros((8, LRU_W), F32)
        h_ref[...] = jnp.zeros_like(h_ref)

    xp_ref[8:8 + tb, :] = x_ref[...]
    rows = lax.broadcasted_iota(I32, (tb, LRU_BW), 0)
    first = (rows == 0) & (i == 0)
    for n in range(LRU_BLOCKS):
        sl = slice(n * LRU_BW, (n + 1) * LRU_BW)
        xc = jnp.broadcast_to(cb_ref[:, sl], (tb, LRU_BW))
        for w in range(CONV_W):
            xc = xc + xp_ref[8 - (CONV_W - 1) + w:8 - (CONV_W - 1) + w + tb, sl] * cw_ref[w:w + 1, sl]
        a, mult, gx = _lru_gates(xc, wa_ref[n], ba_ref[:, sl], wx_ref[n], bx_ref[:, sl], lam_ref[:, sl])
        b = jnp.where(first, 1.0, mult) * gx
        d = 1
        while d < tb:
            a_sh = pltpu.roll(a, d, 0)
            b_sh = pltpu.roll(b, d, 0)
            ok = rows >= d
            b = jnp.where(ok, a * b_sh + b, b)
            a = jnp.where(ok, a * a_sh, a)
            d *= 2
        h = a * h_ref[:, sl] + b
        h_ref[:, sl] = h[tb - 1:tb, :]
        o_ref[:, sl] = h * _gelu_tanh(y_ref[:, sl])
    xp_ref[0:8, :] = xp_ref[tb:tb + 8, :]
    hl_ref[...] = h_ref[...]


def rglru_prompt(z, conv_w, conv_b, wa, ba, wx, bx, lam):
    t = z.shape[0]
    tb = _tile(t, 256)
    vec = lambda a: a.reshape(1, LRU_W)
    return pl.pallas_call(
        functools.partial(_lru_kernel, tb=tb),
        out_shape=[jax.ShapeDtypeStruct((t, LRU_W), F32), jax.ShapeDtypeStruct((1, LRU_W), F32)],
        grid=(t // tb,),
        in_specs=[pl.BlockSpec((tb, LRU_W), lambda i: (i, Z_LX // LRU_W)),
                  pl.BlockSpec((tb, LRU_W), lambda i: (i, Z_LY // LRU_W)),
                  _cst((CONV_W, LRU_W)), _cst((1, LRU_W)),
                  _cst((LRU_BLOCKS, LRU_BW, LRU_BW)), _cst((1, LRU_W)),
                  _cst((LRU_BLOCKS, LRU_BW, LRU_BW)), _cst((1, LRU_W)), _cst((1, LRU_W))],
        out_specs=[pl.BlockSpec((tb, LRU_W), lambda i: (i, 0)), _cst((1, LRU_W))],
        scratch_shapes=[pltpu.VMEM((tb + 8, LRU_W), F32), pltpu.VMEM((1, LRU_W), F32)],
        compiler_params=_cparams(("arbitrary",)),
        name="rglru_prompt",
    )(z, z, conv_w, vec(conv_b), wa.astype(BF16), vec(ba), wx.astype(BF16), vec(bx), vec(lam))


def _lru_sample_kernel(xs_ref, y_ref, h0_ref, cw_ref, cb_ref, wa_ref, ba_ref, wx_ref, bx_ref, lam_ref,
                       o_ref, hl_ref, *, ts, first_is_start):
    h = h0_ref[0]
    for t in range(ts):
        xc = cb_ref[...]
        for w in range(CONV_W):
            xc = xc + xs_ref[t + w] * cw_ref[w:w + 1, :]
        a, mult, gx = _lru_gates(xc, wa_ref[0], ba_ref[...], wx_ref[0], bx_ref[...], lam_ref[...])
        if first_is_start and t == 0:
            mult = jnp.ones_like(mult)
        h = a * h + mult * gx
        o_ref[t] = h * _gelu_tanh(y_ref[t])
    hl_ref[...] = h


def rglru_sample(xs, ly, state_lru, layer, conv_w, conv_b, wa, ba, wx, bx, lam, first_is_start):
    ts, b, w = ly.shape
    vec = lambda a: a.reshape(1, w)
    blk = lambda r: pl.BlockSpec((r, b, LRU_BW), lambda n: (0, 0, n))
    col = lambda r: pl.BlockSpec((r, LRU_BW), lambda n: (0, n))
    wsp = pl.BlockSpec((1, LRU_BW, LRU_BW), lambda n: (n, 0, 0))
    return pl.pallas_call(
        functools.partial(_lru_sample_kernel, ts=ts, first_is_start=first_is_start),
        out_shape=[jax.ShapeDtypeStruct((ts, b, w), F32), jax.ShapeDtypeStruct((b, w), F32)],
        grid=(LRU_BLOCKS,),
        in_specs=[blk(CONV_W - 1 + ts), blk(ts),
                  pl.BlockSpec((1, b, LRU_BW), lambda n: (layer, 0, n)),
                  col(CONV_W), col(1), wsp, col(1), wsp, col(1), col(1)],
        out_specs=[blk(ts), pl.BlockSpec((b, LRU_BW), lambda n: (0, n))],
        compiler_params=_cparams(("parallel",)),
        name="rglru_sample",
    )(xs, ly, state_lru, conv_w, vec(conv_b), wa.astype(BF16), vec(ba), wx.astype(BF16), vec(bx), vec(lam))


def _gla_gate_kernel(glr_ref, w2_ref, bg_ref, o_ref):
    x = _dot(glr_ref[...].astype(BF16), w2_ref[...]) + bg_ref[...]
    o_ref[...] = _log_sigmoid(x) * (1.0 / GLA_TAU)


def gla_gate(z, w2, bg):
    t = z.shape[0]
    w = GLA_HEADS * GLA_DK
    w2p = jnp.pad(w2, ((0, LANE - GLA_RANK), (0, 0))).astype(BF16)
    return pl.pallas_call(
        _gla_gate_kernel,
        out_shape=jax.ShapeDtypeStruct((t, w), F32),
        grid=(1,),
        in_specs=[pl.BlockSpec((t, LANE), lambda i: (0, Z_GLR // LANE)), _cst((LANE, w)), _cst((1, w))],
        out_specs=_cst((t, w)),
        compiler_params=_cparams(("arbitrary",)),
        name="gla_gate",
    )(z, w2p, bg.reshape(1, w))


def _gla_sample_kernel(qt_ref, kt_ref, gt_ref, v_ref, r_ref, s0_ref, ng_ref, o_ref, s_ref, *, ts):
    ng = ng_ref[...]
    for h in range(GLA_HEADS):
        s = s0_ref[0, 0, h]
        qt = qt_ref[0, h] * (GLA_DK ** -0.5)
        kt = kt_ref[0, h]
        dec = jnp.exp(gt_ref[0, h])
        sl = slice(h * GLA_DV, (h + 1) * GLA_DV)
        for t in range(ts):
            s = dec[:, t:t + 1] * s + kt[:, t:t + 1] * v_ref[0, t:t + 1, sl]
            o = jnp.sum(qt[:, t:t + 1] * s, axis=0, keepdims=True)
            on = o * lax.rsqrt(jnp.mean(o * o, -1, keepdims=True) + LN_EPS) * ng
            o_ref[0, t:t + 1, sl] = on * _silu(r_ref[0, t:t + 1, sl])
        s_ref[0, h] = s


def gla_sample(qt, kt, gt, v, r, state_gla, layer, ng):
    b, h, dk, ts = qt.shape
    col = pl.BlockSpec((1, h, dk, ts), lambda i: (i, 0, 0, 0))
    row = pl.BlockSpec((1, ts, h * GLA_DV), lambda i: (i, 0, 0))
    return pl.pallas_call(
        functools.partial(_gla_sample_kernel, ts=ts),
        out_shape=[jax.ShapeDtypeStruct((b, ts, h * GLA_DV), F32),
                   jax.ShapeDtypeStruct((b, h, dk, GLA_DV), F32)],
        grid=(b,),
        in_specs=[col, col, col, row, row,
                  pl.BlockSpec((1, 1, h, dk, GLA_DV), lambda i: (layer, i, 0, 0, 0)), _cst((1, GLA_DV))],
        out_specs=[row, pl.BlockSpec((1, h, dk, GLA_DV), lambda i: (i, 0, 0, 0))],
        compiler_params=_cparams(("parallel",)),
        name="gla_sample",
    )(qt, kt, gt, v, r, state_gla, ng.reshape(1, GLA_DV))


def _mem_heads(q_of, k_of, v_of, store):
    scale = MEM_HD ** -0.5
    for h in range(MEM_HEADS):
        sl = slice(h * MEM_HD, (h + 1) * MEM_HD)
        s = _dot_nt(q_of(sl).astype(BF16), k_of(sl).astype(BF16)) * scale
        p = jnp.exp(s - jnp.max(s, -1, keepdims=True))
        p = p / jnp.sum(p, -1, keepdims=True)
        store(sl, _dot(p.astype(BF16), v_of(sl).astype(BF16)))


def _mem_kernel(q_ref, mk_ref, mv_ref, o_ref):
    def store(sl, val):
        o_ref[:, sl] = val
    _mem_heads(lambda sl: q_ref[:, sl], lambda sl: mk_ref[:, sl], lambda sl: mv_ref[:, sl], store)


def mem_attend_prompt(z, mkv):
    t = z.shape[0]
    tb = _tile(t, 512)
    w = MEM_HEADS * MEM_HD
    return pl.pallas_call(
        _mem_kernel,
        out_shape=jax.ShapeDtypeStruct((t, w), F32),
        grid=(t // tb,),
        in_specs=[pl.BlockSpec((tb, w), lambda i: (i, Z_MQ // w)),
                  pl.BlockSpec((N_MEM, w), lambda i: (0, 0)),
                  pl.BlockSpec((N_MEM, w), lambda i: (0, 1))],
        out_specs=pl.BlockSpec((tb, w), lambda i: (i, 0)),
        compiler_params=_cparams(("parallel",)),
        name="mem_attend",
    )(z, mkv, mkv)


def _mem_sample_kernel(q_ref, mk_ref, mv_ref, o_ref):
    def store(sl, val):
        o_ref[0, :, sl] = val
    head = lambda sl: sl.start // MEM_HD
    _mem_heads(lambda sl: q_ref[0, :, sl], lambda sl: mk_ref[0, 0, :, head(sl), :],
               lambda sl: mv_ref[0, 0, :, head(sl), :], store)


def mem_attend_sample(q, mem_k, mem_v, layer):
    b, rows, w = q.shape
    qs = pl.BlockSpec((1, rows, w), lambda i: (i, 0, 0))
    ms = pl.BlockSpec((1, 1, N_MEM, MEM_HEADS, MEM_HD), lambda i: (layer, i, 0, 0, 0))
    return pl.pallas_call(
        _mem_sample_kernel,
        out_shape=jax.ShapeDtypeStruct((b, rows, w), F32),
        grid=(b,),
        in_specs=[qs, ms, ms],
        out_specs=qs,
        compiler_params=_cparams(("parallel",)),
        name="mem_attend_sample",
    )(q, mem_k, mem_v)


def _merge_kernel(b0_ref, b1_ref, b2_ref, b3_ref, g0_ref, g1_ref, g2_ref, g3_ref, x_ref,
                  wb_ref, bg_ref, wo_ref, lg_ref, lb_ref, o_ref, *, alpha):
    brs = (b0_ref, b1_ref, b2_ref, b3_ref)
    gls = (g0_ref, g1_ref, g2_ref, g3_ref)
    merged = None
    for j in range(N_BRANCH):
        proj = _dot(brs[j][...].astype(BF16), wb_ref[j])
        term = _sigmoid(gls[j][...] + bg_ref[j:j + 1, :]) * proj
        merged = term if merged is None else merged + term
    y = _dot(merged.astype(BF16), wo_ref[...])
    o_ref[...] = _ln_rows(alpha * x_ref[...] + y, lg_ref[...], lb_ref[...])


def merge_out(branches, z, x, wb, bgate, wo, lg, lb, alpha):
    t = x.shape[0]
    tb = _tile(t, 256)
    d = D_MODEL
    row = pl.BlockSpec((tb, d), lambda i: (i, 0))
    gl = [pl.BlockSpec((tb, d), lambda i, o=Z_GL // d + j: (i, o)) for j in range(N_BRANCH)]
    return pl.pallas_call(
        functools.partial(_merge_kernel, alpha=alpha),
        out_shape=jax.ShapeDtypeStruct((t, d), F32),
        grid=(t // tb,),
        in_specs=[row, row, row, row] + gl + [row, _cst((N_BRANCH, d, d)), _cst((N_BRANCH, d)), _cst((d, d)),
                                               _cst((1, d)), _cst((1, d))],
        out_specs=row,
        compiler_params=_cparams(("parallel",)),
        name="merge_out",
    )(*branches, z, z, z, z, x, wb, bgate, wo, lg.reshape(1, d), lb.reshape(1, d))


def _first_argmax(x, n):
    m = jnp.max(x, axis=0, keepdims=True)
    ri = lax.broadcasted_iota(I32, x.shape, 0)
    idx = jnp.min(jnp.where(x == m, ri, n), axis=0, keepdims=True)
    return m, idx, ri


def _router_kernel(x_ref, rwt_ref, rb_ref, e_ref, w_ref, r_ref, c_ref, cnt_ref, tri_ref):
    i = pl.program_id(0)

    @pl.when(i == 0)
    def _():
        cnt_ref[...] = jnp.zeros_like(cnt_ref)
        r = lax.broadcasted_iota(I32, tri_ref.shape, 0)
        c = lax.broadcasted_iota(I32, tri_ref.shape, 1)
        tri_ref[...] = (r < c).astype(BF16)

    logits = lax.dot_general(rwt_ref[...], x_ref[...], (((1,), (1,)), ((), ())),
                             preferred_element_type=F32, precision=lax.Precision.HIGHEST)
    s = _sigmoid(logits)
    sb = s + rb_ref[...]
    gsz = N_EXPERTS // N_GROUPS
    neg = -jnp.inf
    gs = []
    for g in range(N_GROUPS):
        blk = sb[g * gsz:(g + 1) * gsz, :]
        m1, i1, ri = _first_argmax(blk, gsz)
        m2 = jnp.max(jnp.where(ri == i1, neg, blk), axis=0, keepdims=True)
        gs.append(m1 + m2)
    gscore = jnp.concatenate(gs, axis=0)
    gmask = jnp.zeros(gscore.shape, jnp.bool_)
    for _ in range(TOPK_GROUPS):
        _, gi, ri = _first_argmax(gscore, N_GROUPS)
        hit = ri == gi
        gmask = gmask | hit
        gscore = jnp.where(hit, neg, gscore)
    cand = jnp.concatenate(
        [jnp.where(gmask[g:g + 1, :], sb[g * gsz:(g + 1) * gsz, :], neg) for g in range(N_GROUPS)], axis=0)
    idxs, ws, ranks = [], [], []
    run = cnt_ref[...]
    for _ in range(TOP_K):
        _, ei, ri = _first_argmax(cand, N_EXPERTS)
        hit = ri == ei
        idxs.append(ei)
        ws.append(jnp.sum(jnp.where(hit, s, 0.0), axis=0, keepdims=True))
        cand = jnp.where(hit, neg, cand)
        hb = hit.astype(BF16)
        before = run + _dot(hb, tri_ref[...])
        ranks.append(jnp.sum(jnp.where(hit, before, 0.0), axis=0, keepdims=True))
        run = run + jnp.sum(hb.astype(F32), axis=1, keepdims=True)
    cnt_ref[...] = run
    c_ref[...] = run
    w = jnp.concatenate(ws, axis=0)
    e_ref[...] = jnp.concatenate(idxs, axis=0)
    w_ref[...] = w / jnp.sum(w, axis=0, keepdims=True) * ROUTED_SCALE
    r_ref[...] = jnp.concatenate(ranks, axis=0).astype(I32)


def moe_router(x, router_w, router_b):
    t, d = x.shape
    tb = _tile(t, 512)
    if tb % LANE != 0:
        tb = t
    tok = pl.BlockSpec((TOP_K, tb), lambda i: (0, i))
    return pl.pallas_call(
        _router_kernel,
        out_shape=[jax.ShapeDtypeStruct((TOP_K, t), I32), jax.ShapeDtypeStruct((TOP_K, t), F32),
                   jax.ShapeDtypeStruct((TOP_K, t), I32), jax.ShapeDtypeStruct((N_EXPERTS, 1), F32)],
        grid=(t // tb,),
        in_specs=[pl.BlockSpec((tb, d), lambda i: (i, 0)), _cst((N_EXPERTS, d)), _cst((N_EXPERTS, 1))],
        out_specs=[tok, tok, tok, _cst((N_EXPERTS, 1))],
        scratch_shapes=[pltpu.VMEM((N_EXPERTS, 1), F32), pltpu.VMEM((tb, tb), BF16)],
        compiler_params=_cparams(("arbitrary",)),
        name="moe_router",
    )(x, router_w.T, router_b.reshape(N_EXPERTS, 1))


def _dest_kernel(e_ref, r_ref, ps_ref, d_ref):
    ps = ps_ref[...]
    rows = []
    for k in range(TOP_K):
        e = e_ref[k:k + 1, :]
        ri = lax.broadcasted_iota(I32, (N_EXPERTS, e.shape[1]), 0)
        rows.append(jnp.sum(jnp.where(ri == e, ps, 0), axis=0, keepdims=True))
    d_ref[...] = jnp.concatenate(rows, axis=0) + r_ref[...]


def moe_dest(eidx, rank, pstarts):
    k, t = eidx.shape
    tb = _tile(t, 512)
    if tb % LANE != 0:
        tb = t
    tok = pl.BlockSpec((k, tb), lambda i: (0, i))
    return pl.pallas_call(
        _dest_kernel,
        out_shape=jax.ShapeDtypeStruct((k, t), I32),
        grid=(t // tb,),
        in_specs=[tok, tok, _cst((N_EXPERTS, 1))],
        out_specs=tok,
        compiler_params=_cparams(("parallel",)),
        name="moe_dest",
    )(eidx, rank, pstarts.reshape(N_EXPERTS, 1).astype(I32))


def _dispatch_kernel(dest_ref, x_ref, xp_in, xp_out, sem, *, tb):
    del xp_in

    def issue(r, c):
        for k in range(TOP_K):
            pltpu.make_async_copy(x_ref.at[r], xp_out.at[dest_ref[r * TOP_K + k]], sem).start()
        return c

    lax.fori_loop(0, tb, issue, 0)
    for k in range(TOP_K):
        pltpu.make_async_copy(x_ref, xp_out.at[pl.ds(0, tb)], sem).wait()


def _token_tile(t):
    tb = LANE
    while t % tb:
        tb //= 2
    assert tb * TOP_K >= LANE, "token count must be a multiple of 16"
    return tb


def moe_dispatch(x3, dest_flat, p):
    t = x3.shape[0]
    tb = _token_tile(t)
    return pl.pallas_call(
        functools.partial(_dispatch_kernel, tb=tb),
        out_shape=jax.ShapeDtypeStruct((p, SUB, LANE), F32),
        grid=(t // tb,),
        in_specs=[pl.BlockSpec((tb * TOP_K,), lambda i: (i,), memory_space=pltpu.SMEM),
                  pl.BlockSpec((tb, SUB, LANE), lambda i: (i, 0, 0)),
                  pl.BlockSpec(memory_space=pl.ANY)],
        out_specs=pl.BlockSpec(memory_space=pl.ANY),
        scratch_shapes=[pltpu.SemaphoreType.DMA(())],
        input_output_aliases={2: 0},
        compiler_params=_cparams(("arbitrary",)),
        name="moe_dispatch",
    )(dest_flat, x3, jnp.zeros((p, SUB, LANE), F32))


def _experts_kernel(be_ref, nv_ref, x_ref, w1_ref, w3_ref, w2_ref, o_ref, w1b, w3b, w2b, xs_ref):
    i = pl.program_id(0)
    e = be_ref[i]
    prev = be_ref[jnp.maximum(i - 1, 0)]

    @pl.when((i == 0) | (e != prev))
    def _():
        w1b[...] = w1_ref[0, 0].astype(BF16)
        w3b[...] = w3_ref[0, 0].astype(BF16)
        w2b[...] = w2_ref[0, 0].astype(BF16)

    @pl.when(i < nv_ref[0])
    def _():
        for s in range(SUB):
            xs_ref[:, s * LANE:(s + 1) * LANE] = x_ref[:, s, :].astype(BF16)
        x = xs_ref[...]
        h = _silu(_dot(x, w1b[...])) * _dot(x, w3b[...])
        y = _dot(h.astype(BF16), w2b[...])
        for s in range(SUB):
            o_ref[:, s, :] = y[:, s * LANE:(s + 1) * LANE]

    @pl.when(i >= nv_ref[0])
    def _():
        o_ref[...] = jnp.zeros_like(o_ref)


def moe_experts(xp, block_e, n_valid, w1, w3, w2, layer):
    p = xp.shape[0]
    d, de = w1.shape[2], w1.shape[3]
    nb = p // MOE_BLK
    tile = pl.BlockSpec((MOE_BLK, SUB, LANE), lambda i, be, nv: (i, 0, 0))
    grid_spec = pltpu.PrefetchScalarGridSpec(
        num_scalar_prefetch=2,
        grid=(nb,),
        in_specs=[tile,
                  pl.BlockSpec((1, 1, d, de), lambda i, be, nv: (layer, be[i], 0, 0)),
                  pl.BlockSpec((1, 1, d, de), lambda i, be, nv: (layer, be[i], 0, 0)),
                  pl.BlockSpec((1, 1, de, d), lambda i, be, nv: (layer, be[i], 0, 0))],
        out_specs=tile,
        scratch_shapes=[pltpu.VMEM((d, de), BF16), pltpu.VMEM((d, de), BF16), pltpu.VMEM((de, d), BF16),
                        pltpu.VMEM((MOE_BLK, d), BF16)])
    return pl.pallas_call(
        _experts_kernel,
        out_shape=jax.ShapeDtypeStruct((p, SUB, LANE), F32),
        grid_spec=grid_spec,
        compiler_params=_cparams(("arbitrary",)),
        name="moe_experts",
    )(block_e, n_valid, xp, w1, w3, w2)


def _moe_final_kernel(dest_ref, x_ref, w_ref, yp_hbm, w1_ref, w3_ref, w2_ref, lg_ref, lb_ref, o_ref,
                      buf, r_ref, sem, *, alpha, tb):
    def issue(r, c):
        for k in range(TOP_K):
            pltpu.make_async_copy(yp_hbm.at[dest_ref[r * TOP_K + k]], buf.at[k, r], sem).start()
        return c

    lax.fori_loop(0, tb, issue, 0)
    for k in range(TOP_K):
        pltpu.make_async_copy(yp_hbm.at[pl.ds(0, tb)], buf.at[k], sem).wait()
    w = w_ref[...]
    wk = [jnp.broadcast_to(w[:, k:k + 1], (tb, LANE)) for k in range(TOP_K)]
    for s in range(SUB):
        acc = buf[0, :, s, :] * wk[0]
        for k in range(1, TOP_K):
            acc = acc + buf[k, :, s, :] * wk[k]
        r_ref[:, s * LANE:(s + 1) * LANE] = acc
    x = x_ref[...]
    xb = x.astype(BF16)
    h = _silu(_dot(xb, w1_ref[...])) * _dot(xb, w3_ref[...])
    shared = _dot(h.astype(BF16), w2_ref[...])
    o_ref[...] = _ln_rows(alpha * x + (r_ref[...] + shared), lg_ref[...], lb_ref[...])


def moe_final(x, wsel, dest_flat, yp, w1, w3, w2, lg, lb, alpha):
    t, d = x.shape
    tb = _token_tile(t)
    ds = w1.shape[1]
    row = pl.BlockSpec((tb, d), lambda i: (i, 0))
    return pl.pallas_call(
        functools.partial(_moe_final_kernel, alpha=alpha, tb=tb),
        out_shape=jax.ShapeDtypeStruct((t, d), F32),
        grid=(t // tb,),
        in_specs=[pl.BlockSpec((tb * TOP_K,), lambda i: (i,), memory_space=pltpu.SMEM),
                  row, pl.BlockSpec((tb, TOP_K), lambda i: (i, 0)), pl.BlockSpec(memory_space=pl.ANY),
                  _cst((d, ds)), _cst((d, ds)), _cst((ds, d)), _cst((1, d)), _cst((1, d))],
        out_specs=row,
        scratch_shapes=[pltpu.VMEM((TOP_K, tb, SUB, LANE), F32), pltpu.VMEM((tb, d), F32),
                        pltpu.SemaphoreType.DMA(())],
        compiler_params=_cparams(("arbitrary",)),
        name="moe_final",
    )(dest_flat, x, wsel, yp, w1.astype(BF16), w3.astype(BF16), w2.astype(BF16), lg.reshape(1, d), lb.reshape(1, d))


def moe_layer(x, lw, alpha):
    n, d = x.shape
    assert d == SUB * LANE, "a token row is moved as one (8,128) tile"
    eidx_t, wsel_t, rank_t, counts = moe_router(x, lw['router_w'], lw['router_bias'])
    m = n * TOP_K
    nb = (m + N_EXPERTS * (MOE_BLK - 1) + MOE_BLK - 1) // MOE_BLK
    p = nb * MOE_BLK
    counts = counts.reshape(N_EXPERTS).astype(I32)
    pcounts = (counts + MOE_BLK - 1) // MOE_BLK * MOE_BLK
    pends = jnp.cumsum(pcounts)
    pstarts = pends - pcounts
    block_e = jnp.minimum(jnp.searchsorted(pends, jnp.arange(nb) * MOE_BLK, side='right'), N_EXPERTS - 1).astype(I32)
    n_valid = (pends[-1] // MOE_BLK).astype(I32).reshape(1)
    dest_flat = moe_dest(eidx_t, rank_t, pstarts).T.reshape(m)
    xp = moe_dispatch(x.reshape(n, SUB, LANE), dest_flat, p)
    yp = moe_experts(xp, block_e, n_valid, lw['exp_w1'], lw['exp_w3'], lw['exp_w2'], lw['layer'])
    return moe_final(x, wsel_t.T, dest_flat, yp, lw['sh_w1'], lw['sh_w3'], lw['sh_w2'],
                     lw['ln2_g'], lw['ln2_b'], alpha)


def _pack_w_in(w):
    parts, start = [], 0
    for s in IN_SIZES:
        parts.append(w[:, start:start + s])
        start += s
    gq, gk, gv, gr, glr, dq, dk, dv, iq, ik, iw, lx, ly, mq, gl = parts
    d = w.shape[0]
    padc = lambda a, n: jnp.pad(a, ((0, 0), (0, n - a.shape[1])))
    cols = [gq, gk, gv, gr, dq, dk, dv, iq, padc(glr, LANE), padc(jnp.concatenate([ik, iw], 1), LANE),
            jnp.zeros((d, Z_LX - Z_MISC - LANE), w.dtype), lx, ly, mq, gl]
    out = jnp.concatenate(cols, axis=1).astype(BF16)
    assert out.shape[1] == Z_W
    return out


def _sample_mixers(zs, hs_tabs, l, bs, ts, past, cache_k, cache_v, cache_idx_k, cache_mem_k, cache_mem_v,
                   state_gla, state_conv, state_lru, page_table, p):
    n = bs * ts
    q_bf, kd, _, vd, _, qi_bf, ki, _, wi = dsa_prep(zs, hs_tabs, p['idx_ln_g'], p['idx_ln_b'])
    o_dsa = dsa_sample(q_bf, qi_bf, wi, ki, kd, vd, cache_k, cache_v, cache_idx_k, page_table, l, bs, ts)
    glog = gla_gate(zs, p['gla_w_gate2'], p['gla_b_gate'])
    colz = lambda a: a.reshape(bs, ts, GLA_HEADS, GLA_DK).transpose(0, 2, 3, 1)
    qt = colz(zs[:, Z_GQ:Z_GQ + GLA_HEADS * GLA_DK])
    kt = colz(zs[:, Z_GK:Z_GK + GLA_HEADS * GLA_DK])
    gt = colz(glog)
    rowz = lambda off: zs[:, off:off + GLA_HEADS * GLA_DV].reshape(bs, ts, GLA_HEADS * GLA_DV)
    o_gla, s_new = gla_sample(qt, kt, gt, rowz(Z_GV), rowz(Z_GR), state_gla, l, p['gla_norm_g'])
    lx = zs[:, Z_LX:Z_LX + LRU_W].reshape(bs, ts, LRU_W)
    ly = zs[:, Z_LY:Z_LY + LRU_W].reshape(bs, ts, LRU_W)
    xp = jnp.concatenate([state_conv[l], lx], axis=1)
    o_lru, h_new = rglru_sample(xp.transpose(1, 0, 2), ly.transpose(1, 0, 2), state_lru, l,
                                p['conv_w'], p['conv_b'], p['lru_wa'], p['lru_ba'], p['lru_wx'], p['lru_bx'],
                                p['lru_lambda'], first_is_start=(past == 0))
    conv_new = xp[:, xp.shape[1] - (CONV_W - 1):]
    w = MEM_HEADS * MEM_HD
    rows = -(-ts // SQ_ROWS) * SQ_ROWS
    mq = jnp.pad(zs[:, Z_MQ:Z_MQ + w].reshape(bs, ts, w), ((0, 0), (0, rows - ts), (0, 0)))
    o_mem = mem_attend_sample(mq, cache_mem_k, cache_mem_v, l)
    o_mem = o_mem[:, :ts]
    branches = (o_gla.reshape(n, -1), o_dsa, o_lru.transpose(1, 0, 2).reshape(n, -1), o_mem.reshape(n, -1))
    states = (kd.reshape(bs, ts, DSA_KV_HEADS, DSA_HD), vd.reshape(bs, ts, DSA_KV_HEADS, DSA_HD),
              ki.reshape(bs, ts, IDX_D), s_new, conv_new, h_new)
    return branches, states


def kernel(x_prompt, x_sample, cache_k, cache_v, cache_idx_k, cache_mem_k, cache_mem_v, state_gla, state_conv, state_lru, page_table, mem_prompt, ln_in_g, ln_in_b, w_in, b_gate, gla_w_gate2, gla_b_gate, gla_norm_g, idx_ln_g, idx_ln_b, conv_w, conv_b, lru_wa, lru_ba, lru_wx, lru_bx, lru_lambda, mem_w_kv, w_branch, w_out, ln1_g, ln1_b, ln2_g, ln2_b, router_w, router_bias, exp_w1, exp_w3, exp_w2, sh_w1, sh_w3, sh_w2):
    bp, sp_len, d = x_prompt.shape
    assert bp == 1, "prompt group is a single sequence"
    bs, ts, _ = x_sample.shape
    depth = w_in.shape[0]
    alpha = (2 * depth) ** 0.25
    past = page_table.shape[1] * PAGE_SIZE
    tabs_p = rope_tables(jnp.arange(sp_len, dtype=I32))
    tabs_s = rope_tables(jnp.tile(past + jnp.arange(ts, dtype=I32), bs))

    hp = layer_norm_rows(x_prompt.reshape(sp_len, d), ln_in_g, ln_in_b)
    hs = layer_norm_rows(x_sample.reshape(bs * ts, d), ln_in_g, ln_in_b)

    st_p, st_s, mem_new = [], [], []
    for l in range(depth):
        lw = {'ln2_g': ln2_g[l], 'ln2_b': ln2_b[l], 'router_w': router_w[l], 'router_bias': router_bias[l],
              'exp_w1': exp_w1, 'exp_w3': exp_w3, 'exp_w2': exp_w2, 'layer': l,
              'sh_w1': sh_w1[l], 'sh_w3': sh_w3[l], 'sh_w2': sh_w2[l]}
        mp = {'gla_w_gate2': gla_w_gate2[l], 'gla_b_gate': gla_b_gate[l], 'gla_norm_g': gla_norm_g[l],
              'idx_ln_g': idx_ln_g[l], 'idx_ln_b': idx_ln_b[l],
              'conv_w': conv_w[l], 'conv_b': conv_b[l], 'lru_wa': lru_wa[l], 'lru_ba': lru_ba[l],
              'lru_wx': lru_wx[l], 'lru_bx': lru_bx[l], 'lru_lambda': lru_lambda[l]}
        w_in_p = _pack_w_in(w_in[l])
        wb = w_branch[l].astype(BF16)
        wo = w_out[l].astype(BF16)

        zp = matmul(hp, w_in_p)
        mkv = matmul(mem_prompt.reshape(N_MEM, d), mem_w_kv[l].astype(BF16))
        q_bf, kd, k_bf, vd, v_bf, qi_bf, ki, ki_bf, wi = dsa_prep(zp, tabs_p, idx_ln_g[l], idx_ln_b[l])
        o_gla, s_gla = gla_prompt(zp, gla_w_gate2[l], gla_b_gate[l], gla_norm_g[l])
        o_dsa = dsa_prompt(q_bf, qi_bf, wi, ki_bf.T, k_bf.T, v_bf)
        o_lru, h_last = rglru_prompt(zp, conv_w[l], conv_b[l], lru_wa[l], lru_ba[l], lru_wx[l], lru_bx[l], lru_lambda[l])
        o_mem = mem_attend_prompt(zp, mkv)
        xp1 = merge_out((o_gla, o_dsa, o_lru, o_mem), zp, hp, wb, b_gate[l], wo, ln1_g[l], ln1_b[l], alpha)
        lx_p = zp[:, Z_LX:Z_LX + LRU_W]
        conv_p = jnp.concatenate([jnp.zeros((CONV_W - 1, LRU_W), F32), lx_p], 0)[-(CONV_W - 1):]
        st_p.append((kd.reshape(1, sp_len, DSA_KV_HEADS, DSA_HD), vd.reshape(1, sp_len, DSA_KV_HEADS, DSA_HD),
                     ki.reshape(1, sp_len, IDX_D), s_gla[None], conv_p[None], h_last.reshape(1, LRU_W)))
        mem_new.append((mkv[:, :MEM_HEADS * MEM_HD].reshape(1, N_MEM, MEM_HEADS, MEM_HD),
                        mkv[:, MEM_HEADS * MEM_HD:].reshape(1, N_MEM, MEM_HEADS, MEM_HD)))

        zs = matmul(hs, w_in_p)
        brs, sts = _sample_mixers(zs, tabs_s, l, bs, ts, past, cache_k, cache_v, cache_idx_k, cache_mem_k,
                                  cache_mem_v, state_gla, state_conv, state_lru, page_table, mp)
        xs1 = merge_out(brs, zs, hs, wb, b_gate[l], wo, ln1_g[l], ln1_b[l], alpha)
        st_s.append(sts)

        x_all = moe_layer(jnp.concatenate([xp1, xs1], 0), lw, alpha)
        hp, hs = x_all[:sp_len], x_all[sp_len:]

    stk = lambda seq, j: jnp.stack([t[j] for t in seq], axis=0)
    k_p, v_p, ik_p, gla_p, conv_p_, lru_p = [stk(st_p, j) for j in range(6)]
    k_s, v_s, ik_s, gla_s, conv_s, lru_s = [stk(st_s, j) for j in range(6)]
    memk_p, memv_p = stk(mem_new, 0), stk(mem_new, 1)
    return (hp.reshape(1, sp_len, d), hs.reshape(bs, ts, d), k_p, v_p, ik_p, gla_p, conv_p_, lru_p, memk_p, memv_p,
            k_s, v_s, ik_s, gla_s, conv_s, lru_s)
```
